```python
import jax, jax.numpy as jnp
from jax import lax
import numpy as np

D_MODEL = 1024
BATCH = 4
SEQ = 8192
DEPTH = 2

N_MIXERS = 2
N_ATT_LAYERS = (DEPTH + 1) // 2
N_HGRN_LAYERS = DEPTH // 2
FOX_HEADS = 16
FOX_HEAD_DIM = D_MODEL // FOX_HEADS
FOX_Q_BLOCK = 128
HGRN_EXPAND = 128
HGRN_HEADS = D_MODEL // HGRN_EXPAND
HGRN_DK = HGRN_EXPAND
HGRN_DV = D_MODEL // HGRN_HEADS
HGRN_CHUNK = 64
FFN_DIM = 2816
CONV_WIDTH = 3
RMS_EPS = 1e-6

kernel_name = "fox_hgrn2_convffn_hybrid"


def rmsnorm(x, g):
    x32 = x.astype(jnp.float32)
    y = x32 * lax.rsqrt(jnp.mean(x32 * x32, axis=-1, keepdims=True) + RMS_EPS)
    return (y * g.astype(jnp.float32)).astype(x.dtype)


def fox_mixer(h, w_in, b_f, w_out):
    B, S, D = h.shape
    H, hd, QB = FOX_HEADS, FOX_HEAD_DIM, FOX_Q_BLOCK
    nqb = S // QB
    proj = h @ w_in
    q, k, v, fl = jnp.split(proj, [D, 2 * D, 3 * D], axis=-1)
    logf = jax.nn.log_sigmoid((fl + b_f).astype(jnp.float32))
    c = jnp.cumsum(logf, axis=1).transpose(0, 2, 1)
    qb = q.reshape(B, nqb, QB, H, hd).transpose(1, 0, 3, 2, 4)
    k = k.reshape(B, S, H, hd).transpose(0, 2, 1, 3)
    v = v.reshape(B, S, H, hd).transpose(0, 2, 1, 3)
    cqb = c.reshape(B, H, nqb, QB).transpose(2, 0, 1, 3)
    starts = jnp.arange(nqb, dtype=jnp.int32) * QB
    s_pos = jnp.arange(S, dtype=jnp.int32)
    scale = 1.0 / np.sqrt(hd)

    def block(args):
        qi, cqi, st = args
        logits = jnp.einsum('bhqd,bhkd->bhqk', qi, k).astype(jnp.float32) * scale
        logits = logits + (cqi[..., :, None] - c[:, :, None, :])
        t_pos = st + jnp.arange(QB, dtype=jnp.int32)
        logits = jnp.where(s_pos[None, :] <= t_pos[:, None], logits, -jnp.inf)
        p = jax.nn.softmax(logits, axis=-1)
        return jnp.einsum('bhqk,bhkd->bhqd', p.astype(v.dtype), v)

    o = lax.map(block, (qb, cqb, starts))
    o = o.transpose(1, 0, 3, 2, 4).reshape(B, S, D)
    return o @ w_out


def hgrn2_mixer(h, w_in, lb, onorm_g, w_out):
    B, S, D = h.shape
    H, Dk, Dv, C = HGRN_HEADS, HGRN_DK, HGRN_DV, HGRN_CHUNK
    nc = S // C
    proj = h @ w_in
    q, fl, i, g = jnp.split(proj, 4, axis=-1)
    q = jax.nn.silu(q).astype(jnp.float32)
    f = lb + (1.0 - lb) * jax.nn.sigmoid(fl.astype(jnp.float32))
    k = 1.0 - f
    logf = jnp.log(f)

    def to_chunks(t, d):
        return t.reshape(B, nc, C, H, d).transpose(0, 3, 1, 2, 4)

    q, k, logf = to_chunks(q, Dk), to_chunks(k, Dk), to_chunks(logf, Dk)
    iv = to_chunks(i.astype(jnp.float32), Dv)
    b = jnp.cumsum(logf, axis=3)
    b_last = b[..., -1:, :]
    q_dec = q * jnp.exp(b)
    k_inv = k * jnp.exp(-b)
    k_state = k * jnp.exp(b_last - b)
    A = jnp.einsum('bhncd,bhnsd->bhncs', q_dec, k_inv)
    causal = jnp.tril(jnp.ones((C, C), dtype=bool))
    A = jnp.where(causal, A, 0.0)
    o_intra = jnp.einsum('bhncs,bhnsv->bhncv', A, iv)
    U = jnp.einsum('bhnsd,bhnsv->bhndv', k_state, iv)
    decay = jnp.exp(b_last[..., 0, :])

    def step(state, xs):
        dec, u = xs
        return dec[..., None] * state + u, state

    s0 = jnp.zeros((B, H, Dk, Dv), jnp.float32)
    _, s_prev = lax.scan(step, s0, (jnp.moveaxis(decay, 2, 0), jnp.moveaxis(U, 2, 0)))
    s_prev = jnp.moveaxis(s_prev, 0, 2)
    o = o_intra + jnp.einsum('bhncd,bhndv->bhncv', q_dec, s_prev)
    o = o.transpose(0, 2, 3, 1, 4).reshape(B, S, H, Dv)
    o = o * lax.rsqrt(jnp.mean(o * o, axis=-1, keepdims=True) + RMS_EPS)
    o = o * onorm_g.astype(jnp.float32).reshape(H, Dv)
    o = o.reshape(B, S, D).astype(h.dtype) * jax.nn.silu(g)
    return o @ w_out


def conv_ffn(h, w_up, conv_w, conv_b, w_down):
    S = h.shape[1]
    u = h @ w_up
    up = jnp.pad(u, ((0, 0), (CONV_WIDTH - 1, 0), (0, 0)))
    y = conv_b
    for j in range(CONV_WIDTH):
        y = y + conv_w[j] * up[:, j:j + S]
    gate, val = jnp.split(y, 2, axis=-1)
    return (jax.nn.silu(gate) * val) @ w_down


def setup_inputs(seed: int = 0) -> dict:
    key = jax.random.key(seed)
    ks = jax.random.split(key, 16)
    D, F = D_MODEL, FFN_DIM
    nrm = lambda k, shape, s: jax.random.normal(k, shape, jnp.float32) * s
    return {
        "x": nrm(ks[0], (BATCH, SEQ, D), 1.0),
        "att_norm_g": 1.0 + nrm(ks[1], (N_ATT_LAYERS, D), 0.02),
        "att_w_in": nrm(ks[2], (N_ATT_LAYERS, D, 3 * D + FOX_HEADS), D ** -0.5),
        "att_b_f": 3.0 + nrm(ks[3], (N_ATT_LAYERS, FOX_HEADS), 0.5),
        "att_w_out": nrm(ks[4], (N_ATT_LAYERS, D, D), D ** -0.5),
        "hgrn_norm_g": 1.0 + nrm(ks[5], (N_HGRN_LAYERS, D), 0.02),
        "hgrn_w_in": nrm(ks[6], (N_HGRN_LAYERS, D, 4 * D), D ** -0.5),
        "hgrn_lb_logits": nrm(ks[7], (DEPTH, D), 0.1),
        "hgrn_onorm_g": 1.0 + nrm(ks[8], (N_HGRN_LAYERS, D), 0.02),
        "hgrn_w_out": nrm(ks[9], (N_HGRN_LAYERS, D, D), D ** -0.5),
        "ffn_norm_g": 1.0 + nrm(ks[10], (DEPTH, D), 0.02),
        "ffn_w_up": nrm(ks[11], (DEPTH, D, 2 * F), D ** -0.5),
        "ffn_conv_w": nrm(ks[12], (DEPTH, CONV_WIDTH, 2 * F), CONV_WIDTH ** -0.5),
        "ffn_conv_b": nrm(ks[13], (DEPTH, 2 * F), 0.02),
        "ffn_w_down": nrm(ks[14], (DEPTH, F, D), F ** -0.5),
        "final_norm_g": 1.0 + nrm(ks[15], (D,), 0.02),
    }


def reference(x, att_norm_g, att_w_in, att_b_f, att_w_out, hgrn_norm_g, hgrn_w_in,
              hgrn_lb_logits, hgrn_onorm_g, hgrn_w_out, ffn_norm_g, ffn_w_up,
              ffn_conv_w, ffn_conv_b, ffn_w_down, final_norm_g):
    sm = jax.nn.softmax(hgrn_lb_logits.astype(jnp.float32), axis=0)
    lower_bounds = jnp.cumsum(sm, axis=0) - sm[0:1]
    for layer in range(DEPTH):
        j = layer // N_MIXERS
        if layer % N_MIXERS == 0:
            x = x + fox_mixer(rmsnorm(x, att_norm_g[j]), att_w_in[j], att_b_f[j], att_w_out[j])
        else:
            x = x + hgrn2_mixer(rmsnorm(x, hgrn_norm_g[j]), hgrn_w_in[j],
                                lower_bounds[layer], hgrn_onorm_g[j], hgrn_w_out[j])
        x = x + conv_ffn(rmsnorm(x, ffn_norm_g[layer]), ffn_w_up[layer], ffn_conv_w[layer],
                         ffn_conv_b[layer], ffn_w_down[layer])
    return rmsnorm(x, final_norm_g)
```

```python
import functools
import math

import jax
import jax.numpy as jnp
from jax import lax
from jax.experimental import pallas as pl
from jax.experimental.pallas import tpu as pltpu

F32 = jnp.float32
BF16 = jnp.bfloat16

RMS_EPS = 1e-6
FOX_HEADS = 16
FOX_HEAD_DIM = 64
HGRN_HEAD_DIM = 128
HGRN_CHUNK = 64
CONV_WIDTH = 3
LOG2E = math.log2(math.e)

LANES = 128
AUG_LANES = 8
NEG_BIG = -1e30
VMEM_LIMIT = 56 * 1024 * 1024

ROW_TILE = 512
ATTN_TILE = 512
FFN_CHUNK = 256


def _rmsnorm(x, g):
    ms = jnp.mean(x * x, axis=-1, keepdims=True)
    return x * lax.rsqrt(ms + RMS_EPS) * g


def _split3(x):
    hi = x.astype(BF16)
    r = x - hi.astype(F32)
    mid = r.astype(BF16)
    lo = (r - mid.astype(F32)).astype(BF16)
    return hi, mid, lo


def _silu(x):
    return x * (1.0 / (1.0 + jnp.exp(-x)))


def _tile_lanes(x, n):
    return jnp.concatenate([x] * n, axis=1) if n > 1 else x


def _fox_in_kernel(x_ref, g_ref, wqkv_ref, wf_ref, bf_ref, qkv_ref, augq_ref, augk_ref,
                   carry_ref, *, d_model, q_scale):
    ts = x_ref.shape[1]

    @pl.when(pl.program_id(1) == 0)
    def _():
        carry_ref[...] = jnp.zeros_like(carry_ref)

    h = _rmsnorm(x_ref[0], g_ref[...]).astype(BF16)
    for c in range(3):
        acc = jnp.dot(h, wqkv_ref[:, c * d_model:(c + 1) * d_model], preferred_element_type=F32)
        if c == 0:
            acc = acc * q_scale
        qkv_ref[0, :, c * d_model:(c + 1) * d_model] = acc.astype(BF16)

    z = jnp.dot(h, wf_ref[...], preferred_element_type=F32) + bf_ref[...]
    logf = jnp.minimum(z, 0.0) - jnp.log1p(jnp.exp(-jnp.abs(z)))
    lc = logf * LOG2E
    row = lax.broadcasted_iota(jnp.int32, (ts, ts), 0)
    col = lax.broadcasted_iota(jnp.int32, (ts, ts), 1)
    tri = jnp.where(col <= row, 1.0, 0.0).astype(BF16)
    hi, mid, lo = _split3(lc)
    cs = (jnp.dot(tri, hi, preferred_element_type=F32)
          + jnp.dot(tri, mid, preferred_element_type=F32)
          + jnp.dot(tri, lo, preferred_element_type=F32)) + carry_ref[...]
    carry_ref[...] = cs[ts - 1:ts, :]

    c_hi, c_mid, c_lo = (t.astype(F32) for t in _split3(cs))
    j = lax.broadcasted_iota(jnp.int32, (ts, LANES), 1) & (AUG_LANES - 1)
    augq = jnp.where(j == 0, c_hi, jnp.where(j == 1, c_mid, jnp.where(j == 2, c_lo,
                     jnp.where(j < 6, 1.0, 0.0))))
    augk = jnp.where(j < 3, 1.0, jnp.where(j == 3, -c_hi, jnp.where(j == 4, -c_mid,
                     jnp.where(j == 5, -c_lo, 0.0))))
    augq_ref[0] = augq.astype(BF16)
    augk_ref[0] = augk.astype(BF16)


def _fox_in(x, g, wqkv, wf, bf, *, ts):
    B, S, D = x.shape
    q_scale = LOG2E / math.sqrt(FOX_HEAD_DIM)
    const = lambda b, s: (0, 0)
    return pl.pallas_call(
        functools.partial(_fox_in_kernel, d_model=D, q_scale=q_scale),
        grid=(B, S // ts),
        in_specs=[
            pl.BlockSpec((1, ts, D), lambda b, s: (b, s, 0)),
            pl.BlockSpec((1, D), const),
            pl.BlockSpec((D, 3 * D), const),
            pl.BlockSpec((D, LANES), const),
            pl.BlockSpec((1, LANES), const),
        ],
        out_specs=[
            pl.BlockSpec((1, ts, 3 * D), lambda b, s: (b, s, 0)),
            pl.BlockSpec((1, ts, LANES), lambda b, s: (b, s, 0)),
            pl.BlockSpec((1, ts, LANES), lambda b, s: (b, s, 0)),
        ],
        out_shape=[
            jax.ShapeDtypeStruct((B, S, 3 * D), BF16),
            jax.ShapeDtypeStruct((B, S, LANES), BF16),
            jax.ShapeDtypeStruct((B, S, LANES), BF16),
        ],
        scratch_shapes=[pltpu.VMEM((1, LANES), F32)],
        compiler_params=pltpu.CompilerParams(
            dimension_semantics=("arbitrary", "arbitrary"), vmem_limit_bytes=VMEM_LIMIT),
        name="fox_in",
    )(x, g, wqkv, wf, bf)


def _fox_attn_kernel(q_ref, augq_ref, k_ref, v_ref, augk_ref, o_ref, lhs_ref, m_ref, acc_ref,
                     *, t):
    j = pl.program_id(1)
    i = pl.program_id(2)
    lane = lax.broadcasted_iota(jnp.int32, (t, LANES), 1)
    q2 = q_ref[0]
    aq = augq_ref[0]
    zero = jnp.zeros((), BF16)
    for hh in range(2):
        lo = (2 * j + hh) * AUG_LANES
        qm = jnp.where((lane >= FOX_HEAD_DIM * hh) & (lane < FOX_HEAD_DIM * (hh + 1)), q2, zero)
        am = jnp.where((lane >= lo) & (lane < lo + AUG_LANES), aq, zero)
        lhs_ref[hh] = jnp.concatenate([qm, am], axis=1)
        m_ref[hh] = jnp.full((t, LANES), NEG_BIG, F32)
        acc_ref[hh] = jnp.zeros((t, 2 * LANES), F32)

    ones = jnp.ones((t, LANES), BF16)
    row = lax.broadcasted_iota(jnp.int32, (t, t), 0)
    col = lax.broadcasted_iota(jnp.int32, (t, t), 1)

    def block(kk, masked):
        off = pl.multiple_of(kk * t, t)
        rhs = jnp.concatenate([k_ref[0, pl.ds(off, t), :], augk_ref[0, pl.ds(off, t), :]], axis=1)
        v2 = jnp.concatenate([v_ref[0, pl.ds(off, t), :], ones], axis=1)
        for hh in range(2):
            s = lax.dot_general(lhs_ref[hh], rhs, (((1,), (1,)), ((), ())),
                                preferred_element_type=F32)
            if masked:
                s = jnp.where(col <= row, s, NEG_BIG)
            m_prev = m_ref[hh]
            m_new = jnp.maximum(m_prev, jnp.max(s, axis=1, keepdims=True))
            alpha = jnp.exp2(m_prev - m_new)
            p = jnp.exp2(s - _tile_lanes(m_new, t // LANES)).astype(BF16)
            pv = jnp.dot(p, v2, preferred_element_type=F32)
            acc_ref[hh] = _tile_lanes(alpha, 2) * acc_ref[hh] + pv
            m_ref[hh] = m_new

    def body(kk, carry):
        block(kk, False)
        return carry

    lax.fori_loop(0, i, body, 0)
    block(i, True)

    out_a = acc_ref[0, :, :LANES] / acc_ref[0, :, LANES:]
    out_b = acc_ref[1, :, :LANES] / acc_ref[1, :, LANES:]
    o_ref[0] = jnp.where(lane < FOX_HEAD_DIM, out_a, out_b).astype(BF16)


def _fox_attn(qkv, augq, augk, *, t):
    B, S, D3 = qkv.shape
    D = D3 // 3
    nb = D // LANES
    return pl.pallas_call(
        functools.partial(_fox_attn_kernel, t=t),
        grid=(B, nb, S // t),
        in_specs=[
            pl.BlockSpec((1, t, LANES), lambda b, j, i: (b, i, j)),
            pl.BlockSpec((1, t, LANES), lambda b, j, i: (b, i, 0)),
            pl.BlockSpec((1, S, LANES), lambda b, j, i: (b, 0, nb + j)),
            pl.BlockSpec((1, S, LANES), lambda b, j, i: (b, 0, 2 * nb + j)),
            pl.BlockSpec((1, S, LANES), lambda b, j, i: (b, 0, 0)),
        ],
        out_specs=pl.BlockSpec((1, t, LANES), lambda b, j, i: (b, i, j)),
        out_shape=jax.ShapeDtypeStruct((B, S, D), BF16),
        scratch_shapes=[
            pltpu.VMEM((2, t, 2 * LANES), BF16),
            pltpu.VMEM((2, t, LANES), F32),
            pltpu.VMEM((2, t, 2 * LANES), F32),
        ],
        compiler_params=pltpu.CompilerParams(
            dimension_semantics=("arbitrary", "arbitrary", "arbitrary"),
            vmem_limit_bytes=VMEM_LIMIT),
        name="fox_attn",
    )(qkv, augq, qkv, qkv, augk)


def _ffn_kernel(x_ref, o_ref, wout_ref, g_ref, wg_ref, wv_ref, cg_ref, cv_ref, wd_ref, fg_ref,
                out_ref, tail_ref, x1_ref, h_ref, acc_ref, *, final):
    ts = x_ref.shape[1]
    n_chunks = wg_ref.shape[0]

    @pl.when(pl.program_id(1) == 0)
    def _():
        tail_ref[...] = jnp.zeros_like(tail_ref)

    x1 = x_ref[0] + jnp.dot(o_ref[0], wout_ref[...], preferred_element_type=F32)
    x1_ref[...] = x1
    h_ref[...] = _rmsnorm(x1, g_ref[...]).astype(BF16)
    acc_ref[...] = jnp.zeros_like(acc_ref)
    row = lax.broadcasted_iota(jnp.int32, (ts, FFN_CHUNK), 0)

    def conv(u, cw, tail):
        t1 = tail[7:8, :]
        t2 = tail[6:7, :]
        u1 = jnp.where(row == 0, t1, pltpu.roll(u, 1, axis=0))
        u2 = jnp.where(row == 0, t2, jnp.where(row == 1, t1, pltpu.roll(u, 2, axis=0)))
        return cw[3:4, :] + cw[0:1, :] * u2 + cw[1:2, :] * u1 + cw[2:3, :] * u

    def chunk(c, carry):
        h = h_ref[...]
        ug = jnp.dot(h, wg_ref[c], preferred_element_type=F32)
        uv = jnp.dot(h, wv_ref[c], preferred_element_type=F32)
        gate = conv(ug, cg_ref[c], tail_ref[0, c])
        val = conv(uv, cv_ref[c], tail_ref[1, c])
        tail_ref[0, c] = ug[ts - 8:, :]
        tail_ref[1, c] = uv[ts - 8:, :]
        z = (_silu(gate) * val).astype(BF16)
        acc_ref[...] += jnp.dot(z, wd_ref[c], preferred_element_type=F32)
        return carry

    lax.fori_loop(0, n_chunks, chunk, 0)
    y = x1_ref[...] + acc_ref[...]
    if final:
        y = _rmsnorm(y, fg_ref[...])
    out_ref[0] = y


def _ffn(x, o, wout, g, wg, wv, cg, cv, wd, fg, *, ts, final):
    B, S, D = x.shape
    n_chunks = wg.shape[0]
    c2 = lambda b, s: (0, 0)
    c3 = lambda b, s: (0, 0, 0)
    return pl.pallas_call(
        functools.partial(_ffn_kernel, final=final),
        grid=(B, S // ts),
        in_specs=[
            pl.BlockSpec((1, ts, D), lambda b, s: (b, s, 0)),
            pl.BlockSpec((1, ts, D), lambda b, s: (b, s, 0)),
            pl.BlockSpec((D, D), c2),
            pl.BlockSpec((1, D), c2),
            pl.BlockSpec(wg.shape, c3),
            pl.BlockSpec(wv.shape, c3),
            pl.BlockSpec(cg.shape, c3),
            pl.BlockSpec(cv.shape, c3),
            pl.BlockSpec(wd.shape, c3),
            pl.BlockSpec((1, D), c2),
        ],
        out_specs=pl.BlockSpec((1, ts, D), lambda b, s: (b, s, 0)),
        out_shape=jax.ShapeDtypeStruct((B, S, D), F32),
        scratch_shapes=[
            pltpu.VMEM((2, n_chunks, 8, FFN_CHUNK), F32),
            pltpu.VMEM((ts, D), F32),
            pltpu.VMEM((ts, D), BF16),
            pltpu.VMEM((ts, D), F32),
        ],
        compiler_params=pltpu.CompilerParams(
            dimension_semantics=("arbitrary", "arbitrary"), vmem_limit_bytes=VMEM_LIMIT),
        name="ffn_final" if final else "ffn",
    )(x, o, wout, g, wg, wv, cg, cv, wd, fg)


def _ffn_weights(w_up, conv_w, conv_b, w_down):
    D, F2 = w_up.shape
    F = F2 // 2
    n = F // FFN_CHUNK

    def cols(w):
        return w.reshape(D, n, FFN_CHUNK).transpose(1, 0, 2).astype(BF16)

    def taps(w, b):
        t = jnp.concatenate([w, b[None, :], jnp.zeros((8 - CONV_WIDTH - 1, F), F32)], axis=0)
        return t.reshape(8, n, FFN_CHUNK).transpose(1, 0, 2)

    return (cols(w_up[:, :F]), cols(w_up[:, F:]),
            taps(conv_w[:, :F], conv_b[:F]), taps(conv_w[:, F:], conv_b[F:]),
            w_down.reshape(n, FFN_CHUNK, D).astype(BF16))


def _hgrn_in_kernel(x_ref, g_ref, w_ref, lb_ref, q_ref, lf_ref, k_ref, i_ref, sg_ref, *, d_model):
    h = _rmsnorm(x_ref[0], g_ref[...]).astype(BF16)
    D = d_model
    q = jnp.dot(h, w_ref[:, 0:D], preferred_element_type=F32)
    q_ref[0] = _silu(q).astype(BF16)
    fl = jnp.dot(h, w_ref[:, D:2 * D], preferred_element_type=F32)
    lb = lb_ref[...]
    f = lb + (1.0 - lb) * (1.0 / (1.0 + jnp.exp(-fl)))
    lf_ref[0] = jnp.log(f)
    k_ref[0] = (1.0 - f).astype(BF16)
    i_ref[0] = jnp.dot(h, w_ref[:, 2 * D:3 * D], preferred_element_type=F32).astype(BF16)
    gg = jnp.dot(h, w_ref[:, 3 * D:4 * D], preferred_element_type=F32)
    sg_ref[0] = _silu(gg).astype(BF16)


def _hgrn_in(x, g, w, lb, *, ts):
    B, S, D = x.shape
    const = lambda b, s: (0, 0)
    tile = pl.BlockSpec((1, ts, D), lambda b, s: (b, s, 0))
    return pl.pallas_call(
        functools.partial(_hgrn_in_kernel, d_model=D),
        grid=(B, S // ts),
        in_specs=[tile, pl.BlockSpec((1, D), const), pl.BlockSpec((D, 4 * D), const),
                  pl.BlockSpec((1, D), const)],
        out_specs=[tile] * 5,
        out_shape=[
            jax.ShapeDtypeStruct((B, S, D), BF16),
            jax.ShapeDtypeStruct((B, S, D), F32),
            jax.ShapeDtypeStruct((B, S, D), BF16),
            jax.ShapeDtypeStruct((B, S, D), BF16),
            jax.ShapeDtypeStruct((B, S, D), BF16),
        ],
        compiler_params=pltpu.CompilerParams(
            dimension_semantics=("arbitrary", "arbitrary"), vmem_limit_bytes=VMEM_LIMIT),
        name="hgrn_in",
    )(x, g, w, lb)


def _hgrn_core_kernel(q_ref, lf_ref, k_ref, i_ref, sg_ref, og_ref, o_ref, st_ref):
    ts = q_ref.shape[1]
    C = HGRN_CHUNK

    @pl.when(pl.program_id(2) == 0)
    def _():
        st_ref[...] = jnp.zeros_like(st_ref)

    row = lax.broadcasted_iota(jnp.int32, (ts, ts), 0)
    col = lax.broadcasted_iota(jnp.int32, (ts, ts), 1)
    tri = jnp.where((col <= row) & (col // C == row // C), 1.0, 0.0).astype(BF16)
    hi, mid, lo = _split3(lf_ref[0])
    b_all = (jnp.dot(tri, hi, preferred_element_type=F32)
             + jnp.dot(tri, mid, preferred_element_type=F32)
             + jnp.dot(tri, lo, preferred_element_type=F32))
    crow = lax.broadcasted_iota(jnp.int32, (C, C), 0)
    ccol = lax.broadcasted_iota(jnp.int32, (C, C), 1)

    for n in range(ts // C):
        sl = slice(n * C, (n + 1) * C)
        b = b_all[sl, :]
        b_last = b[C - 1:C, :]
        q = q_ref[0, sl, :].astype(F32)
        k = k_ref[0, sl, :].astype(F32)
        iv = i_ref[0, sl, :]
        q_dec = (q * jnp.exp(b)).astype(BF16)
        k_inv = (k * jnp.exp(-b)).astype(BF16)
        k_state = (k * jnp.exp(b_last - b)).astype(BF16)
        a = lax.dot_general(q_dec, k_inv, (((1,), (1,)), ((), ())), preferred_element_type=F32)
        a = jnp.where(ccol <= crow, a, 0.0).astype(BF16)
        st = st_ref[...]
        o = jnp.dot(a, iv, preferred_element_type=F32)
        o = o + lax.dot_general(q_dec, st.astype(BF16), (((1,), (1,)), ((), ())),
                                preferred_element_type=F32)
        ut = lax.dot_general(iv, k_state, (((0,), (0,)), ((), ())), preferred_element_type=F32)
        st_ref[...] = jnp.exp(b_last) * st + ut
        o = o * lax.rsqrt(jnp.mean(o * o, axis=-1, keepdims=True) + RMS_EPS)
        o = o * og_ref[...] * sg_ref[0, sl, :].astype(F32)
        o_ref[0, sl, :] = o.astype(BF16)


def _hgrn_core(q, lf, k, iv, sg, og, *, ts):
    B, S, D = q.shape
    H = D // HGRN_HEAD_DIM
    tile = pl.BlockSpec((1, ts, HGRN_HEAD_DIM), lambda b, h, s: (b, s, h))
    return pl.pallas_call(
        _hgrn_core_kernel,
        grid=(B, H, S // ts),
        in_specs=[tile] * 5 + [pl.BlockSpec((1, HGRN_HEAD_DIM), lambda b, h, s: (0, h))],
        out_specs=tile,
        out_shape=jax.ShapeDtypeStruct((B, S, D), BF16),
        scratch_shapes=[pltpu.VMEM((HGRN_HEAD_DIM, HGRN_HEAD_DIM), F32)],
        compiler_params=pltpu.CompilerParams(
            dimension_semantics=("arbitrary", "arbitrary", "arbitrary"),
            vmem_limit_bytes=VMEM_LIMIT),
        name="hgrn_core",
    )(q, lf, k, iv, sg, og)


def kernel(x, att_norm_g, att_w_in, att_b_f, att_w_out, hgrn_norm_g, hgrn_w_in, hgrn_lb_logits,
           hgrn_onorm_g, hgrn_w_out, ffn_norm_g, ffn_w_up, ffn_conv_w, ffn_conv_b, ffn_w_down,
           final_norm_g):
    B, S, D = x.shape
    depth = ffn_norm_g.shape[0]
    assert D == FOX_HEADS * FOX_HEAD_DIM and FOX_HEADS * AUG_LANES == LANES
    assert S % ROW_TILE == 0 and S % ATTN_TILE == 0 and ROW_TILE % HGRN_CHUNK == 0
    assert (ffn_w_up.shape[2] // 2) % FFN_CHUNK == 0

    sm = jax.nn.softmax(hgrn_lb_logits.astype(F32), axis=0)
    lower_bounds = jnp.cumsum(sm, axis=0) - sm[0:1]
    fg = final_norm_g.reshape(1, D)

    for layer in range(depth):
        j = layer // 2
        if layer % 2 == 0:
            w_in = att_w_in[j]
            wqkv = w_in[:, :3 * D].astype(BF16)
            wf = jnp.repeat(w_in[:, 3 * D:], AUG_LANES, axis=1).astype(BF16)
            bf = jnp.repeat(att_b_f[j], AUG_LANES).reshape(1, LANES)
            qkv, augq, augk = _fox_in(x, att_norm_g[j].reshape(1, D), wqkv, wf, bf, ts=ROW_TILE)
            o = _fox_attn(qkv, augq, augk, t=ATTN_TILE)
            w_out = att_w_out[j]
        else:
            q, lf, k, iv, sg = _hgrn_in(x, hgrn_norm_g[j].reshape(1, D), hgrn_w_in[j].astype(BF16),
                                        lower_bounds[layer].reshape(1, D), ts=ROW_TILE)
            o = _hgrn_core(q, lf, k, iv, sg, hgrn_onorm_g[j].reshape(1, D), ts=ROW_TILE)
            w_out = hgrn_w_out[j]
        wg, wv, cg, cv, wd = _ffn_weights(ffn_w_up[layer], ffn_conv_w[layer], ffn_conv_b[layer],
                                          ffn_w_down[layer])
        x = _ffn(x, o, w_out.astype(BF16), ffn_norm_g[layer].reshape(1, D), wg, wv, cg, cv, wd, fg,
                 ts=ROW_TILE, final=(layer == depth - 1))
    return x
```

```python
import functools
import math

import jax
import jax.numpy as jnp
from jax import lax
from jax.experimental import pallas as pl
from jax.experimental.pallas import tpu as pltpu

F32 = jnp.float32
BF16 = jnp.bfloat16

RMS_EPS = 1e-6
FOX_HEADS = 16
FOX_HEAD_DIM = 64
HGRN_HEAD_DIM = 128
HGRN_CHUNK = 64
CONV_WIDTH = 3
LOG2E = math.log2(math.e)

LANES = 128
AUG_LANES = 8
NEG_BIG = -1e30
VMEM_LIMIT = 56 * 1024 * 1024

ROW_TILE = 512
ATTN_TILE = 512
FFN_CHUNK = 256


def _rmsnorm(x, g):
    ms = jnp.mean(x * x, axis=-1, keepdims=True)
    return x * lax.rsqrt(ms + RMS_EPS) * g


def _split3(x):
    hi = x.astype(BF16)
    r = x - hi.astype(F32)
    mid = r.astype(BF16)
    lo = (r - mid.astype(F32)).astype(BF16)
    return hi, mid, lo


def _silu(x):
    return x * (1.0 / (1.0 + jnp.exp(-x)))


def _tile_lanes(x, n):
    return jnp.concatenate([x] * n, axis=1) if n > 1 else x


def _fox_in_kernel(x_ref, g_ref, wqkv_ref, wf_ref, bf_ref, qkv_ref, augq_ref, augk_ref,
                   carry_ref, *, d_model, q_scale):
    ts = x_ref.shape[1]

    @pl.when(pl.program_id(1) == 0)
    def _():
        carry_ref[...] = jnp.zeros_like(carry_ref)

    h = _rmsnorm(x_ref[0], g_ref[...]).astype(BF16)
    for c in range(3):
        acc = jnp.dot(h, wqkv_ref[:, c * d_model:(c + 1) * d_model], preferred_element_type=F32)
        if c == 0:
            acc = acc * q_scale
        qkv_ref[0, :, c * d_model:(c + 1) * d_model] = acc.astype(BF16)

    z = jnp.dot(h, wf_ref[...], preferred_element_type=F32) + bf_ref[...]
    logf = jnp.minimum(z, 0.0) - jnp.log1p(jnp.exp(-jnp.abs(z)))
    lc = logf * LOG2E
    row = lax.broadcasted_iota(jnp.int32, (ts, ts), 0)
    col = lax.broadcasted_iota(jnp.int32, (ts, ts), 1)
    tri = jnp.where(col <= row, 1.0, 0.0).astype(BF16)
    hi, mid, lo = _split3(lc)
    cs = (jnp.dot(tri, hi, preferred_element_type=F32)
          + jnp.dot(tri, mid, preferred_element_type=F32)
          + jnp.dot(tri, lo, preferred_element_type=F32)) + carry_ref[...]
    carry_ref[...] = cs[ts - 1:ts, :]

    c_hi, c_mid, c_lo = (t.astype(F32) for t in _split3(cs))
    j = lax.broadcasted_iota(jnp.int32, (ts, LANES), 1) & (AUG_LANES - 1)
    augq = jnp.where(j == 0, c_hi, jnp.where(j == 1, c_mid, jnp.where(j == 2, c_lo,
                     jnp.where(j < 6, 1.0, 0.0))))
    augk = jnp.where(j < 3, 1.0, jnp.where(j == 3, -c_hi, jnp.where(j == 4, -c_mid,
                     jnp.where(j == 5, -c_lo, 0.0))))
    augq_ref[0] = augq.astype(BF16)
    augk_ref[0] = augk.astype(BF16)


def _fox_in(x, g, wqkv, wf, bf, *, ts):
    B, S, D = x.shape
    q_scale = LOG2E / math.sqrt(FOX_HEAD_DIM)
    const = lambda b, s: (0, 0)
    return pl.pallas_call(
        functools.partial(_fox_in_kernel, d_model=D, q_scale=q_scale),
        grid=(B, S // ts),
        in_specs=[
            pl.BlockSpec((1, ts, D), lambda b, s: (b, s, 0)),
            pl.BlockSpec((1, D), const),
            pl.BlockSpec((D, 3 * D), const),
            pl.BlockSpec((D, LANES), const),
            pl.BlockSpec((1, LANES), const),
        ],
        out_specs=[
            pl.BlockSpec((1, ts, 3 * D), lambda b, s: (b, s, 0)),
            pl.BlockSpec((1, ts, LANES), lambda b, s: (b, s, 0)),
            pl.BlockSpec((1, ts, LANES), lambda b, s: (b, s, 0)),
        ],
        out_shape=[
            jax.ShapeDtypeStruct((B, S, 3 * D), BF16),
            jax.ShapeDtypeStruct((B, S, LANES), BF16),
            jax.ShapeDtypeStruct((B, S, LANES), BF16),
        ],
        scratch_shapes=[pltpu.VMEM((1, LANES), F32)],
        compiler_params=pltpu.CompilerParams(
            dimension_semantics=("arbitrary", "arbitrary"), vmem_limit_bytes=VMEM_LIMIT),
        name="fox_in",
    )(x, g, wqkv, wf, bf)


def _fox_attn_kernel(q_ref, augq_ref, k_ref, v_ref, augk_ref, o_ref, lhs_ref, m_ref, acc_ref,
                     sa_ref, sb_ref, pa_ref, pb_ref, ala_ref, alb_ref, mxa_ref, mxb_ref, *, t):
    j = pl.program_id(1)
    i = pl.program_id(2)
    lane = lax.broadcasted_iota(jnp.int32, (t, LANES), 1)
    q2 = q_ref[0]
    aq = augq_ref[0]
    zero = jnp.zeros((), BF16)
    for hh in range(2):
        lo = (2 * j + hh) * AUG_LANES
        qm = jnp.where((lane >= FOX_HEAD_DIM * hh) & (lane < FOX_HEAD_DIM * (hh + 1)), q2, zero)
        am = jnp.where((lane >= lo) & (lane < lo + AUG_LANES), aq, zero)
        lhs_ref[hh] = jnp.concatenate([qm, am], axis=1)
        m_ref[hh] = jnp.full((t, LANES), NEG_BIG, F32)
        acc_ref[hh] = jnp.zeros((t, 2 * LANES), F32)

    ones = jnp.ones((t, LANES), BF16)
    row = lax.broadcasted_iota(jnp.int32, (t, t), 0)
    col = lax.broadcasted_iota(jnp.int32, (t, t), 1)

    def scores(jb, buf, masked=False):
        s_ref, mx_ref, _, _ = buf
        off = pl.multiple_of((i - jb) * t, t)
        rhs = jnp.concatenate([k_ref[0, pl.ds(off, t), :], augk_ref[0, pl.ds(off, t), :]], axis=1)
        for hh in range(2):
            s = lax.dot_general(lhs_ref[hh], rhs, (((1,), (1,)), ((), ())),
                                preferred_element_type=F32)
            if masked:
                s = jnp.where(col <= row, s, NEG_BIG)
            s_ref[hh] = s
            mx = s[:, :LANES]
            for c in range(1, t // LANES):
                mx = jnp.maximum(mx, s[:, c * LANES:(c + 1) * LANES])
            mx_ref[hh] = mx

    def softmax(buf):
        s_ref, mx_ref, p_ref, al_ref = buf
        for hh in range(2):
            m_prev = m_ref[hh]
            m_new = jnp.maximum(m_prev, jnp.max(mx_ref[hh], axis=1, keepdims=True))
            al_ref[hh] = jnp.exp2(m_prev - m_new)
            p_ref[hh] = jnp.exp2(s_ref[hh] - _tile_lanes(m_new, t // LANES)).astype(BF16)
            m_ref[hh] = m_new

    def values(jb, buf):
        _, _, p_ref, al_ref = buf
        off = pl.multiple_of((i - jb) * t, t)
        v2 = jnp.concatenate([v_ref[0, pl.ds(off, t), :], ones], axis=1)
        for hh in range(2):
            pv = jnp.dot(p_ref[hh], v2, preferred_element_type=F32)
            acc_ref[hh] = _tile_lanes(al_ref[hh], 2) * acc_ref[hh] + pv

    n = i + 1
    n_steady = jnp.maximum(n - 2, 0)

    buf_a = (sa_ref, mxa_ref, pa_ref, ala_ref)
    buf_b = (sb_ref, mxb_ref, pb_ref, alb_ref)

    scores(0, buf_a, masked=True)

    @pl.when(n > 1)
    def _():
        softmax(buf_a)
        scores(1, buf_b)

    @pl.when(n == 1)
    def _():
        softmax(buf_a)

    def step_pair(u, carry):
        tau = 2 + 2 * u
        values(tau - 2, buf_a)
        softmax(buf_b)
        scores(tau, buf_a)
        values(tau - 1, buf_b)
        softmax(buf_a)
        scores(tau + 1, buf_b)
        return carry

    lax.fori_loop(0, n_steady // 2, step_pair, 0)

    @pl.when(n_steady % 2 == 1)
    def _():
        values(n - 3, buf_a)
        softmax(buf_b)
        scores(n - 1, buf_a)

    @pl.when(n % 2 == 0)
    def _():
        values(n - 2, buf_a)
        softmax(buf_b)
        values(n - 1, buf_b)

    @pl.when((n % 2 == 1) & (n >= 3))
    def _():
        values(n - 2, buf_b)
        softmax(buf_a)
        values(n - 1, buf_a)

    @pl.when(n == 1)
    def _():
        values(0, buf_a)

    out_a = acc_ref[0, :, :LANES] / acc_ref[0, :, LANES:]
    out_b = acc_ref[1, :, :LANES] / acc_ref[1, :, LANES:]
    o_ref[0] = jnp.where(lane < FOX_HEAD_DIM, out_a, out_b).astype(BF16)


def _fox_attn(qkv, augq, augk, *, t):
    B, S, D3 = qkv.shape
    D = D3 // 3
    nb = D // LANES
    return pl.pallas_call(
        functools.partial(_fox_attn_kernel, t=t),
        grid=(B, nb, S // t),
        in_specs=[
            pl.BlockSpec((1, t, LANES), lambda b, j, i: (b, i, j)),
            pl.BlockSpec((1, t, LANES), lambda b, j, i: (b, i, 0)),
            pl.BlockSpec((1, S, LANES), lambda b, j, i: (b, 0, nb + j)),
            pl.BlockSpec((1, S, LANES), lambda b, j, i: (b, 0, 2 * nb + j)),
            pl.BlockSpec((1, S, LANES), lambda b, j, i: (b, 0, 0)),
        ],
        out_specs=pl.BlockSpec((1, t, LANES), lambda b, j, i: (b, i, j)),
        out_shape=jax.ShapeDtypeStruct((B, S, D), BF16),
        scratch_shapes=[
            pltpu.VMEM((2, t, 2 * LANES), BF16),
            pltpu.VMEM((2, t, LANES), F32),
            pltpu.VMEM((2, t, 2 * LANES), F32),
            pltpu.VMEM((2, t, t), F32),
            pltpu.VMEM((2, t, t), F32),
            pltpu.VMEM((2, t, t), BF16),
            pltpu.VMEM((2, t, t), BF16),
            pltpu.VMEM((2, t, LANES), F32),
            pltpu.VMEM((2, t, LANES), F32),
            pltpu.VMEM((2, t, LANES), F32),
            pltpu.VMEM((2, t, LANES), F32),
        ],
        compiler_params=pltpu.CompilerParams(
            dimension_semantics=("arbitrary", "arbitrary", "arbitrary"),
            vmem_limit_bytes=VMEM_LIMIT),
        name="fox_attn",
    )(qkv, augq, qkv, qkv, augk)


def _ffn_kernel(x_ref, o_ref, wout_ref, g_ref, wg_ref, wv_ref, cg_ref, cv_ref, wd_ref, fg_ref,
                out_ref, tail_ref, x1_ref, h_ref, acc_ref, *, final):
    ts = x_ref.shape[1]
    n_chunks = wg_ref.shape[0]

    @pl.when(pl.program_id(1) == 0)
    def _():
        tail_ref[...] = jnp.zeros_like(tail_ref)

    x1 = x_ref[0] + jnp.dot(o_ref[0], wout_ref[...], preferred_element_type=F32)
    x1_ref[...] = x1
    h_ref[...] = _rmsnorm(x1, g_ref[...]).astype(BF16)
    acc_ref[...] = jnp.zeros_like(acc_ref)
    row = lax.broadcasted_iota(jnp.int32, (ts, FFN_CHUNK), 0)

    def conv(u, cw, tail):
        t1 = tail[7:8, :]
        t2 = tail[6:7, :]
        u1 = jnp.where(row == 0, t1, pltpu.roll(u, 1, axis=0))
        u2 = jnp.where(row == 0, t2, jnp.where(row == 1, t1, pltpu.roll(u, 2, axis=0)))
        return cw[3:4, :] + cw[0:1, :] * u2 + cw[1:2, :] * u1 + cw[2:3, :] * u

    def chunk(c, carry):
        h = h_ref[...]
        ug = jnp.dot(h, wg_ref[c], preferred_element_type=F32)
        uv = jnp.dot(h, wv_ref[c], preferred_element_type=F32)
        gate = conv(ug, cg_ref[c], tail_ref[0, c])
        val = conv(uv, cv_ref[c], tail_ref[1, c])
        tail_ref[0, c] = ug[ts - 8:, :]
        tail_ref[1, c] = uv[ts - 8:, :]
        z = (_silu(gate) * val).astype(BF16)
        acc_ref[...] += jnp.dot(z, wd_ref[c], preferred_element_type=F32)
        return carry

    lax.fori_loop(0, n_chunks, chunk, 0)
    y = x1_ref[...] + acc_ref[...]
    if final:
        y = _rmsnorm(y, fg_ref[...])
    out_ref[0] = y


def _ffn(x, o, wout, g, wg, wv, cg, cv, wd, fg, *, ts, final):
    B, S, D = x.shape
    n_chunks = wg.shape[0]
    c2 = lambda b, s: (0, 0)
    c3 = lambda b, s: (0, 0, 0)
    return pl.pallas_call(
        functools.partial(_ffn_kernel, final=final),
        grid=(B, S // ts),
        in_specs=[
            pl.BlockSpec((1, ts, D), lambda b, s: (b, s, 0)),
            pl.BlockSpec((1, ts, D), lambda b, s: (b, s, 0)),
            pl.BlockSpec((D, D), c2),
            pl.BlockSpec((1, D), c2),
            pl.BlockSpec(wg.shape, c3),
            pl.BlockSpec(wv.shape, c3),
            pl.BlockSpec(cg.shape, c3),
            pl.BlockSpec(cv.shape, c3),
            pl.BlockSpec(wd.shape, c3),
            pl.BlockSpec((1, D), c2),
        ],
        out_specs=pl.BlockSpec((1, ts, D), lambda b, s: (b, s, 0)),
        out_shape=jax.ShapeDtypeStruct((B, S, D), F32),
        scratch_shapes=[
            pltpu.VMEM((2, n_chunks, 8, FFN_CHUNK), F32),
            pltpu.VMEM((ts, D), F32),
            pltpu.VMEM((ts, D), BF16),
            pltpu.VMEM((ts, D), F32),
        ],
        compiler_params=pltpu.CompilerParams(
            dimension_semantics=("arbitrary", "arbitrary"), vmem_limit_bytes=VMEM_LIMIT),
        name="ffn_final" if final else "ffn",
    )(x, o, wout, g, wg, wv, cg, cv, wd, fg)


def _ffn_weights(w_up, conv_w, conv_b, w_down):
    D, F2 = w_up.shape
    F = F2 // 2
    n = F // FFN_CHUNK

    def cols(w):
        return w.reshape(D, n, FFN_CHUNK).transpose(1, 0, 2).astype(BF16)

    def taps(w, b):
        t = jnp.concatenate([w, b[None, :], jnp.zeros((8 - CONV_WIDTH - 1, F), F32)], axis=0)
        return t.reshape(8, n, FFN_CHUNK).transpose(1, 0, 2)

    return (cols(w_up[:, :F]), cols(w_up[:, F:]),
            taps(conv_w[:, :F], conv_b[:F]), taps(conv_w[:, F:], conv_b[F:]),
            w_down.reshape(n, FFN_CHUNK, D).astype(BF16))


def _hgrn_in_kernel(x_ref, g_ref, w_ref, lb_ref, q_ref, lf_ref, k_ref, i_ref, sg_ref, *, d_model):
    h = _rmsnorm(x_ref[0], g_ref[...]).astype(BF16)
    D = d_model
    q = jnp.dot(h, w_ref[:, 0:D], preferred_element_type=F32)
    q_ref[0] = _silu(q).astype(BF16)
    fl = jnp.dot(h, w_ref[:, D:2 * D], preferred_element_type=F32)
    lb = lb_ref[...]
    f = lb + (1.0 - lb) * (1.0 / (1.0 + jnp.exp(-fl)))
    lf_ref[0] = jnp.log(f)
    k_ref[0] = (1.0 - f).astype(BF16)
    i_ref[0] = jnp.dot(h, w_ref[:, 2 * D:3 * D], preferred_element_type=F32).astype(BF16)
    gg = jnp.dot(h, w_ref[:, 3 * D:4 * D], preferred_element_type=F32)
    sg_ref[0] = _silu(gg).astype(BF16)


def _hgrn_in(x, g, w, lb, *, ts):
    B, S, D = x.shape
    const = lambda b, s: (0, 0)
    tile = pl.BlockSpec((1, ts, D), lambda b, s: (b, s, 0))
    return pl.pallas_call(
        functools.partial(_hgrn_in_kernel, d_model=D),
        grid=(B, S // ts),
        in_specs=[tile, pl.BlockSpec((1, D), const), pl.BlockSpec((D, 4 * D), const),
                  pl.BlockSpec((1, D), const)],
        out_specs=[tile] * 5,
        out_shape=[
            jax.ShapeDtypeStruct((B, S, D), BF16),
            jax.ShapeDtypeStruct((B, S, D), F32),
            jax.ShapeDtypeStruct((B, S, D), BF16),
            jax.ShapeDtypeStruct((B, S, D), BF16),
            jax.ShapeDtypeStruct((B, S, D), BF16),
        ],
        compiler_params=pltpu.CompilerParams(
            dimension_semantics=("arbitrary", "arbitrary"), vmem_limit_bytes=VMEM_LIMIT),
        name="hgrn_in",
    )(x, g, w, lb)


def _chunk_cumsum(x):
    r = lax.broadcasted_iota(jnp.int32, x.shape, 0) & (HGRN_CHUNK - 1)
    k = 1
    while k < HGRN_CHUNK:
        x = x + jnp.where(r >= k, pltpu.roll(x, k, axis=0), 0.0)
        k *= 2
    return x


def _hgrn_core_kernel(q_ref, lf_ref, k_ref, i_ref, sg_ref, og_ref, o_ref, st_ref):
    ts = q_ref.shape[1]
    C = HGRN_CHUNK

    @pl.when(pl.program_id(2) == 0)
    def _():
        st_ref[...] = jnp.zeros_like(st_ref)

    b_all = _chunk_cumsum(lf_ref[0])
    q_all = q_ref[0].astype(F32)
    k_all = k_ref[0].astype(F32)
    q_dec_all = (q_all * jnp.exp(b_all)).astype(BF16)
    k_inv_all = (k_all * jnp.exp(-b_all)).astype(BF16)
    crow = lax.broadcasted_iota(jnp.int32, (C, C), 0)
    ccol = lax.broadcasted_iota(jnp.int32, (C, C), 1)

    st = st_ref[...]
    outs = []
    for n in range(ts // C):
        sl = slice(n * C, (n + 1) * C)
        b_last = b_all[(n + 1) * C - 1:(n + 1) * C, :]
        iv = i_ref[0, sl, :]
        q_dec = q_dec_all[sl, :]
        k_state = (k_all[sl, :] * jnp.exp(b_last - b_all[sl, :])).astype(BF16)
        a = lax.dot_general(q_dec, k_inv_all[sl, :], (((1,), (1,)), ((), ())),
                            preferred_element_type=F32)
        a = jnp.where(ccol <= crow, a, 0.0).astype(BF16)
        o = jnp.dot(a, iv, preferred_element_type=F32)
        o = o + lax.dot_general(q_dec, st.astype(BF16), (((1,), (1,)), ((), ())),
                                preferred_element_type=F32)
        ut = lax.dot_general(iv, k_state, (((0,), (0,)), ((), ())), preferred_element_type=F32)
        st = jnp.exp(b_last) * st + ut
        outs.append(o)
    st_ref[...] = st
    o = jnp.concatenate(outs, axis=0)
    o = o * lax.rsqrt(jnp.mean(o * o, axis=-1, keepdims=True) + RMS_EPS)
    o = o * og_ref[...] * sg_ref[0].astype(F32)
    o_ref[0] = o.astype(BF16)


def _hgrn_core(q, lf, k, iv, sg, og, *, ts):
    B, S, D = q.shape
    H = D // HGRN_HEAD_DIM
    tile = pl.BlockSpec((1, ts, HGRN_HEAD_DIM), lambda b, h, s: (b, s, h))
    return pl.pallas_call(
        _hgrn_core_kernel,
        grid=(B, H, S // ts),
        in_specs=[tile] * 5 + [pl.BlockSpec((1, HGRN_HEAD_DIM), lambda b, h, s: (0, h))],
        out_specs=tile,
        out_shape=jax.ShapeDtypeStruct((B, S, D), BF16),
        scratch_shapes=[pltpu.VMEM((HGRN_HEAD_DIM, HGRN_HEAD_DIM), F32)],
        compiler_params=pltpu.CompilerParams(
            dimension_semantics=("arbitrary", "arbitrary", "arbitrary"),
            vmem_limit_bytes=VMEM_LIMIT),
        name="hgrn_core",
    )(q, lf, k, iv, sg, og)


def kernel(x, att_norm_g, att_w_in, att_b_f, att_w_out, hgrn_norm_g, hgrn_w_in, hgrn_lb_logits,
           hgrn_onorm_g, hgrn_w_out, ffn_norm_g, ffn_w_up, ffn_conv_w, ffn_conv_b, ffn_w_down,
           final_norm_g):
    B, S, D = x.shape
    depth = ffn_norm_g.shape[0]
    assert D == FOX_HEADS * FOX_HEAD_DIM and FOX_HEADS * AUG_LANES == LANES
    assert S % ROW_TILE == 0 and S % ATTN_TILE == 0 and ROW_TILE % HGRN_CHUNK == 0
    assert (ffn_w_up.shape[2] // 2) % FFN_CHUNK == 0

    sm = jax.nn.softmax(hgrn_lb_logits.astype(F32), axis=0)
    lower_bounds = jnp.cumsum(sm, axis=0) - sm[0:1]
    fg = final_norm_g.reshape(1, D)

    for layer in range(depth):
        j = layer // 2
        if layer % 2 == 0:
            w_in = att_w_in[j]
            wqkv = w_in[:, :3 * D].astype(BF16)
            wf = jnp.repeat(w_in[:, 3 * D:], AUG_LANES, axis=1).astype(BF16)
            bf = jnp.repeat(att_b_f[j], AUG_LANES).reshape(1, LANES)
            qkv, augq, augk = _fox_in(x, att_norm_g[j].reshape(1, D), wqkv, wf, bf, ts=ROW_TILE)
            o = _fox_attn(qkv, augq, augk, t=ATTN_TILE)
            w_out = att_w_out[j]
        else:
            q, lf, k, iv, sg = _hgrn_in(x, hgrn_norm_g[j].reshape(1, D), hgrn_w_in[j].astype(BF16),
                                        lower_bounds[layer].reshape(1, D), ts=ROW_TILE)
            o = _hgrn_core(q, lf, k, iv, sg, hgrn_onorm_g[j].reshape(1, D), ts=ROW_TILE)
            w_out = hgrn_w_out[j]
        wg, wv, cg, cv, wd = _ffn_weights(ffn_w_up[layer], ffn_conv_w[layer], ffn_conv_b[layer],
                                          ffn_w_down[layer])
        x = _ffn(x, o, w_out.astype(BF16), ffn_norm_g[layer].reshape(1, D), wg, wv, cg, cv, wd, fg,
                 ts=ROW_TILE, final=(layer == depth - 1))
    return x
```

```python
import functools
import math

import jax
import jax.numpy as jnp
from jax import lax
from jax.experimental import pallas as pl
from jax.experimental.pallas import tpu as pltpu

F32 = jnp.float32
BF16 = jnp.bfloat16

RMS_EPS = 1e-6
FOX_HEADS = 16
FOX_HEAD_DIM = 64
HGRN_HEAD_DIM = 128
HGRN_CHUNK = 64
CONV_WIDTH = 3
LOG2E = math.log2(math.e)

LANES = 128
AUG_LANES = 8
NEG_BIG = -1e30
NORM_SLACK = 1.01
EXP2_ZERO_GAP = 160.0
VMEM_LIMIT = 56 * 1024 * 1024

ROW_TILE = 512
ATTN_TILE = 512
HGRN_TILE = 1024
FFN_CHUNK = 256


def _rmsnorm(x, g):
    ms = jnp.mean(x * x, axis=-1, keepdims=True)
    return x * lax.rsqrt(ms + RMS_EPS) * g


def _split3(x):
    hi = x.astype(BF16)
    r = x - hi.astype(F32)
    mid = r.astype(BF16)
    lo = (r - mid.astype(F32)).astype(BF16)
    return hi, mid, lo


def _silu(x):
    return x * (1.0 / (1.0 + jnp.exp(-x)))


def _tile_lanes(x, n):
    return jnp.concatenate([x] * n, axis=1) if n > 1 else x


def _fox_in_kernel(x_ref, g_ref, wqkv_ref, wf_ref, bf_ref, hsum_ref, qkv_ref, augq_ref, augk_ref,
                   stats_ref, carry_ref, *, d_model, q_scale):
    ts = x_ref.shape[1]

    @pl.when(pl.program_id(1) == 0)
    def _():
        carry_ref[...] = jnp.zeros_like(carry_ref)

    h = _rmsnorm(x_ref[0], g_ref[...]).astype(BF16)
    norm2_max = []
    for c in range(3):
        acc = jnp.dot(h, wqkv_ref[:, c * d_model:(c + 1) * d_model], preferred_element_type=F32)
        if c == 0:
            acc = acc * q_scale
        yb = acc.astype(BF16)
        qkv_ref[0, :, c * d_model:(c + 1) * d_model] = yb
        if c < 2:
            sq = jnp.square(yb.astype(F32)).astype(BF16)
            n2 = jnp.dot(sq, hsum_ref[...], preferred_element_type=F32) * NORM_SLACK
            norm2_max.append(jnp.max(n2, axis=0, keepdims=True))

    z = jnp.dot(h, wf_ref[...], preferred_element_type=F32) + bf_ref[...]
    logf = jnp.minimum(z, 0.0) - jnp.log1p(jnp.exp(-jnp.abs(z)))
    lc = logf * LOG2E
    row = lax.broadcasted_iota(jnp.int32, (ts, ts), 0)
    col = lax.broadcasted_iota(jnp.int32, (ts, ts), 1)
    tri = jnp.where(col <= row, 1.0, 0.0).astype(BF16)
    hi, mid, lo = _split3(lc)
    cs = (jnp.dot(tri, hi, preferred_element_type=F32)
          + jnp.dot(tri, mid, preferred_element_type=F32)
          + jnp.dot(tri, lo, preferred_element_type=F32)) + carry_ref[...]
    carry_ref[...] = cs[ts - 1:ts, :]

    c_hi, c_mid, c_lo = (t.astype(F32) for t in _split3(cs))
    j = lax.broadcasted_iota(jnp.int32, (ts, LANES), 1) & (AUG_LANES - 1)
    augq = jnp.where(j == 0, c_hi, jnp.where(j == 1, c_mid, jnp.where(j == 2, c_lo,
                     jnp.where(j < 6, 1.0, 0.0))))
    augk = jnp.where(j < 3, 1.0, jnp.where(j == 3, -c_hi, jnp.where(j == 4, -c_mid,
                     jnp.where(j == 5, -c_lo, 0.0))))
    augq_ref[0] = augq.astype(BF16)
    augk_ref[0] = augk.astype(BF16)

    stats_ref[0, 0] = jnp.concatenate(
        [norm2_max[0], norm2_max[1], cs[0:1, :], cs[ts - 1:ts, :], jnp.zeros((4, LANES), F32)],
        axis=0)


def _fox_in(x, g, wqkv, wf, bf, hsum, *, ts):
    B, S, D = x.shape
    q_scale = LOG2E / math.sqrt(FOX_HEAD_DIM)
    const = lambda b, s: (0, 0)
    return pl.pallas_call(
        functools.partial(_fox_in_kernel, d_model=D, q_scale=q_scale),
        grid=(B, S // ts),
        in_specs=[
            pl.BlockSpec((1, ts, D), lambda b, s: (b, s, 0)),
            pl.BlockSpec((1, D), const),
            pl.BlockSpec((D, 3 * D), const),
            pl.BlockSpec((D, LANES), const),
            pl.BlockSpec((1, LANES), const),
            pl.BlockSpec((D, LANES), const),
        ],
        out_specs=[
            pl.BlockSpec((1, ts, 3 * D), lambda b, s: (b, s, 0)),
            pl.BlockSpec((1, ts, LANES), lambda b, s: (b, s, 0)),
            pl.BlockSpec((1, ts, LANES), lambda b, s: (b, s, 0)),
            pl.BlockSpec((1, 1, 8, LANES), lambda b, s: (b, s, 0, 0)),
        ],
        out_shape=[
            jax.ShapeDtypeStruct((B, S, 3 * D), BF16),
            jax.ShapeDtypeStruct((B, S, LANES), BF16),
            jax.ShapeDtypeStruct((B, S, LANES), BF16),
            jax.ShapeDtypeStruct((B, S // ts, 8, LANES), F32),
        ],
        scratch_shapes=[pltpu.VMEM((1, LANES), F32)],
        compiler_params=pltpu.CompilerParams(
            dimension_semantics=("arbitrary", "arbitrary"), vmem_limit_bytes=VMEM_LIMIT),
        name="fox_in",
    )(x, g, wqkv, wf, bf, hsum)


def _attn_block_counts(stats):
    B, nt = stats.shape[0], stats.shape[1]
    per_head = stats[:, :, :4, ::AUG_LANES]
    qn2, kn2, c_first, c_last = (per_head[:, :, r] for r in range(4))
    kn2_max = jnp.max(kn2, axis=1, keepdims=True)
    bound = 2.0 * jnp.sqrt(qn2 * kn2_max) + c_first + 1.0
    gap = bound[:, :, None, :] - c_last[:, None, :, :]
    kk = lax.broadcasted_iota(jnp.int32, (nt, nt), 1)
    ii = lax.broadcasted_iota(jnp.int32, (nt, nt), 0)
    need = (gap > -EXP2_ZERO_GAP) & (kk < ii)[None, :, :, None]
    oldest = jnp.min(jnp.where(need, kk[None, :, :, None], nt), axis=2)
    count = jnp.maximum(jnp.arange(nt)[None, :, None] - oldest, 0) + 1
    pairs = jnp.max(count.reshape(B, nt, -1, 2), axis=-1)
    return jnp.transpose(pairs, (0, 2, 1)).reshape(-1).astype(jnp.int32)


def _fox_attn_kernel(nblk_ref, q_ref, augq_ref, k_ref, v_ref, augk_ref, o_ref, lhs_ref, m_ref,
                     acc_ref, sa_ref, sb_ref, pa_ref, pb_ref, ala_ref, alb_ref, mxa_ref, mxb_ref,
                     *, t):
    j = pl.program_id(1)
    i = pl.program_id(2)
    n = nblk_ref[(pl.program_id(0) * pl.num_programs(1) + j) * pl.num_programs(2) + i]
    lane = lax.broadcasted_iota(jnp.int32, (t, LANES), 1)
    q2 = q_ref[0]
    aq = augq_ref[0]
    zero = jnp.zeros((), BF16)
    for hh in range(2):
        lo = (2 * j + hh) * AUG_LANES
        qm = jnp.where((lane >= FOX_HEAD_DIM * hh) & (lane < FOX_HEAD_DIM * (hh + 1)), q2, zero)
        am = jnp.where((lane >= lo) & (lane < lo + AUG_LANES), aq, zero)
        lhs_ref[hh] = jnp.concatenate([qm, am], axis=1)
        m_ref[hh] = jnp.full((t, LANES), NEG_BIG, F32)
        acc_ref[hh] = jnp.zeros((t, 2 * LANES), F32)

    ones = jnp.ones((t, LANES), BF16)
    row = lax.broadcasted_iota(jnp.int32, (t, t), 0)
    col = lax.broadcasted_iota(jnp.int32, (t, t), 1)

    def scores(jb, buf, masked=False):
        s_ref, mx_ref, _, _ = buf
        off = pl.multiple_of((i - jb) * t, t)
        rhs = jnp.concatenate([k_ref[0, pl.ds(off, t), :], augk_ref[0, pl.ds(off, t), :]], axis=1)
        for hh in range(2):
            s = lax.dot_general(lhs_ref[hh], rhs, (((1,), (1,)), ((), ())),
                                preferred_element_type=F32)
            if masked:
                s = jnp.where(col <= row, s, NEG_BIG)
            s_ref[hh] = s
            mx = s[:, :LANES]
            for c in range(1, t // LANES):
                mx = jnp.maximum(mx, s[:, c * LANES:(c + 1) * LANES])
            mx_ref[hh] = mx

    def softmax(buf):
        s_ref, mx_ref, p_ref, al_ref = buf
        for hh in range(2):
            m_prev = m_ref[hh]
            m_new = jnp.maximum(m_prev, jnp.max(mx_ref[hh], axis=1, keepdims=True))
            al_ref[hh] = jnp.exp2(m_prev - m_new)
            p_ref[hh] = jnp.exp2(s_ref[hh] - _tile_lanes(m_new, t // LANES)).astype(BF16)
            m_ref[hh] = m_new

    def values(jb, buf):
        _, _, p_ref, al_ref = buf
        off = pl.multiple_of((i - jb) * t, t)
        v2 = jnp.concatenate([v_ref[0, pl.ds(off, t), :], ones], axis=1)
        for hh in range(2):
            pv = jnp.dot(p_ref[hh], v2, preferred_element_type=F32)
            acc_ref[hh] = _tile_lanes(al_ref[hh], 2) * acc_ref[hh] + pv

    n_steady = jnp.maximum(n - 2, 0)

    buf_a = (sa_ref, mxa_ref, pa_ref, ala_ref)
    buf_b = (sb_ref, mxb_ref, pb_ref, alb_ref)

    scores(0, buf_a, masked=True)

    @pl.when(n > 1)
    def _():
        softmax(buf_a)
        scores(1, buf_b)

    @pl.when(n == 1)
    def _():
        softmax(buf_a)

    def step_pair(u, carry):
        tau = 2 + 2 * u
        values(tau - 2, buf_a)
        softmax(buf_b)
        scores(tau, buf_a)
        values(tau - 1, buf_b)
        softmax(buf_a)
        scores(tau + 1, buf_b)
        return carry

    lax.fori_loop(0, n_steady // 2, step_pair, 0)

    @pl.when(n_steady % 2 == 1)
    def _():
        values(n - 3, buf_a)
        softmax(buf_b)
        scores(n - 1, buf_a)

    @pl.when(n % 2 == 0)
    def _():
        values(n - 2, buf_a)
        softmax(buf_b)
        values(n - 1, buf_b)

    @pl.when((n % 2 == 1) & (n >= 3))
    def _():
        values(n - 2, buf_b)
        softmax(buf_a)
        values(n - 1, buf_a)

    @pl.when(n == 1)
    def _():
        values(0, buf_a)

    out_a = acc_ref[0, :, :LANES] / acc_ref[0, :, LANES:]
    out_b = acc_ref[1, :, :LANES] / acc_ref[1, :, LANES:]
    o_ref[0] = jnp.where(lane < FOX_HEAD_DIM, out_a, out_b).astype(BF16)


def _fox_attn(nblk, qkv, augq, augk, *, t):
    B, S, D3 = qkv.shape
    D = D3 // 3
    nb = D // LANES
    grid_spec = pltpu.PrefetchScalarGridSpec(
        num_scalar_prefetch=1,
        grid=(B, nb, S // t),
        in_specs=[
            pl.BlockSpec((1, t, LANES), lambda b, j, i, nblk: (b, i, j)),
            pl.BlockSpec((1, t, LANES), lambda b, j, i, nblk: (b, i, 0)),
            pl.BlockSpec((1, S, LANES), lambda b, j, i, nblk: (b, 0, nb + j)),
            pl.BlockSpec((1, S, LANES), lambda b, j, i, nblk: (b, 0, 2 * nb + j)),
            pl.BlockSpec((1, S, LANES), lambda b, j, i, nblk: (b, 0, 0)),
        ],
        out_specs=pl.BlockSpec((1, t, LANES), lambda b, j, i, nblk: (b, i, j)),
        scratch_shapes=[
            pltpu.VMEM((2, t, 2 * LANES), BF16),
            pltpu.VMEM((2, t, LANES), F32),
            pltpu.VMEM((2, t, 2 * LANES), F32),
            pltpu.VMEM((2, t, t), F32),
            pltpu.VMEM((2, t, t), F32),
            pltpu.VMEM((2, t, t), BF16),
            pltpu.VMEM((2, t, t), BF16),
            pltpu.VMEM((2, t, LANES), F32),
            pltpu.VMEM((2, t, LANES), F32),
            pltpu.VMEM((2, t, LANES), F32),
            pltpu.VMEM((2, t, LANES), F32),
        ],
    )
    return pl.pallas_call(
        functools.partial(_fox_attn_kernel, t=t),
        grid_spec=grid_spec,
        out_shape=jax.ShapeDtypeStruct((B, S, D), BF16),
        compiler_params=pltpu.CompilerParams(
            dimension_semantics=("arbitrary", "arbitrary", "arbitrary"),
            vmem_limit_bytes=VMEM_LIMIT),
        name="fox_attn",
    )(nblk, qkv, augq, qkv, qkv, augk)


def _ffn_kernel(x_ref, o_ref, wout_ref, g_ref, wg_ref, wv_ref, cg_ref, cv_ref, wd_ref, fg_ref,
                out_ref, tail_ref, x1_ref, h_ref, acc_ref, u_ref, *, final):
    ts = x_ref.shape[1]
    n_chunks = wg_ref.shape[0]

    @pl.when(pl.program_id(1) == 0)
    def _():
        tail_ref[...] = jnp.zeros_like(tail_ref)

    x1 = x_ref[0] + jnp.dot(o_ref[0], wout_ref[...], preferred_element_type=F32)
    x1_ref[...] = x1
    h_ref[...] = _rmsnorm(x1, g_ref[...]).astype(BF16)
    acc_ref[...] = jnp.zeros_like(acc_ref)
    def up(c):
        h = h_ref[...]
        for stream, w_ref in enumerate((wg_ref, wv_ref)):
            u = jnp.dot(h, w_ref[c], preferred_element_type=F32)
            u_ref[c % 2, stream, 0:8, :] = tail_ref[stream, c]
            u_ref[c % 2, stream, 8:, :] = u
            tail_ref[stream, c] = u[ts - 8:, :]

    def conv(c, stream, cw):
        u0 = u_ref[c % 2, stream, 8:8 + ts, :]
        u1 = u_ref[c % 2, stream, 7:7 + ts, :]
        u2 = u_ref[c % 2, stream, 6:6 + ts, :]
        return cw[3:4, :] + cw[0:1, :] * u2 + cw[1:2, :] * u1 + cw[2:3, :] * u0

    def down(c):
        gate = conv(c, 0, cg_ref[c])
        val = conv(c, 1, cv_ref[c])
        z = (_silu(gate) * val).astype(BF16)
        acc_ref[...] += jnp.dot(z, wd_ref[c], preferred_element_type=F32)

    up(0)
    for c in range(n_chunks):
        if c + 1 < n_chunks:
            up(c + 1)
        down(c)
    y = x1_ref[...] + acc_ref[...]
    if final:
        y = _rmsnorm(y, fg_ref[...])
    out_ref[0] = y


def _ffn(x, o, wout, g, wg, wv, cg, cv, wd, fg, *, ts, final):
    B, S, D = x.shape
    n_chunks = wg.shape[0]
    c2 = lambda b, s: (0, 0)
    c3 = lambda b, s: (0, 0, 0)
    return pl.pallas_call(
        functools.partial(_ffn_kernel, final=final),
        grid=(B, S // ts),
        in_specs=[
            pl.BlockSpec((1, ts, D), lambda b, s: (b, s, 0)),
            pl.BlockSpec((1, ts, D), lambda b, s: (b, s, 0)),
            pl.BlockSpec((D, D), c2),
            pl.BlockSpec((1, D), c2),
            pl.BlockSpec(wg.shape, c3),
            pl.BlockSpec(wv.shape, c3),
            pl.BlockSpec(cg.shape, c3),
            pl.BlockSpec(cv.shape, c3),
            pl.BlockSpec(wd.shape, c3),
            pl.BlockSpec((1, D), c2),
        ],
        out_specs=pl.BlockSpec((1, ts, D), lambda b, s: (b, s, 0)),
        out_shape=jax.ShapeDtypeStruct((B, S, D), F32),
        scratch_shapes=[
            pltpu.VMEM((2, n_chunks, 8, FFN_CHUNK), F32),
            pltpu.VMEM((ts, D), F32),
            pltpu.VMEM((ts, D), BF16),
            pltpu.VMEM((ts, D), F32),
            pltpu.VMEM((2, 2, 8 + ts, FFN_CHUNK), F32),
        ],
        compiler_params=pltpu.CompilerParams(
            dimension_semantics=("arbitrary", "arbitrary"), vmem_limit_bytes=VMEM_LIMIT),
        name="ffn_final" if final else "ffn",
    )(x, o, wout, g, wg, wv, cg, cv, wd, fg)


def _ffn_weights(w_up, conv_w, conv_b, w_down):
    D, F2 = w_up.shape
    F = F2 // 2
    n = F // FFN_CHUNK

    def cols(w):
        return w.reshape(D, n, FFN_CHUNK).transpose(1, 0, 2).astype(BF16)

    def taps(w, b):
        t = jnp.concatenate([w, b[None, :], jnp.zeros((8 - CONV_WIDTH - 1, F), F32)], axis=0)
        return t.reshape(8, n, FFN_CHUNK).transpose(1, 0, 2)

    return (cols(w_up[:, :F]), cols(w_up[:, F:]),
            taps(conv_w[:, :F], conv_b[:F]), taps(conv_w[:, F:], conv_b[F:]),
            w_down.reshape(n, FFN_CHUNK, D).astype(BF16))


def _hgrn_in_kernel(x_ref, g_ref, w_ref, lb_ref, q_ref, lf_ref, k_ref, i_ref, sg_ref, *, d_model):
    h = _rmsnorm(x_ref[0], g_ref[...]).astype(BF16)
    D = d_model
    q = jnp.dot(h, w_ref[:, 0:D], preferred_element_type=F32)
    q_ref[0] = _silu(q).astype(BF16)
    fl = jnp.dot(h, w_ref[:, D:2 * D], preferred_element_type=F32)
    lb = lb_ref[...]
    f = lb + (1.0 - lb) * (1.0 / (1.0 + jnp.exp(-fl)))
    lf_ref[0] = jnp.log(f)
    k_ref[0] = (1.0 - f).astype(BF16)
    i_ref[0] = jnp.dot(h, w_ref[:, 2 * D:3 * D], preferred_element_type=F32).astype(BF16)
    gg = jnp.dot(h, w_ref[:, 3 * D:4 * D], preferred_element_type=F32)
    sg_ref[0] = _silu(gg).astype(BF16)


def _hgrn_in(x, g, w, lb, *, ts):
    B, S, D = x.shape
    const = lambda b, s: (0, 0)
    tile = pl.BlockSpec((1, ts, D), lambda b, s: (b, s, 0))
    return pl.pallas_call(
        functools.partial(_hgrn_in_kernel, d_model=D),
        grid=(B, S // ts),
        in_specs=[tile, pl.BlockSpec((1, D), const), pl.BlockSpec((D, 4 * D), const),
                  pl.BlockSpec((1, D), const)],
        out_specs=[tile] * 5,
        out_shape=[
            jax.ShapeDtypeStruct((B, S, D), BF16),
            jax.ShapeDtypeStruct((B, S, D), F32),
            jax.ShapeDtypeStruct((B, S, D), BF16),
            jax.ShapeDtypeStruct((B, S, D), BF16),
            jax.ShapeDtypeStruct((B, S, D), BF16),
        ],
        compiler_params=pltpu.CompilerParams(
            dimension_semantics=("arbitrary", "arbitrary"), vmem_limit_bytes=VMEM_LIMIT),
        name="hgrn_in",
    )(x, g, w, lb)


def _chunk_cumsum(x):
    r = lax.broadcasted_iota(jnp.int32, x.shape, 0) & (HGRN_CHUNK - 1)
    k = 1
    while k < HGRN_CHUNK:
        x = x + jnp.where(r >= k, pltpu.roll(x, k, axis=0), 0.0)
        k *= 2
    return x


def _hgrn_core_kernel(q_ref, lf_ref, k_ref, i_ref, sg_ref, og_ref, o_ref, st_ref):
    ts = q_ref.shape[1]
    C = HGRN_CHUNK

    @pl.when(pl.program_id(2) == 0)
    def _():
        st_ref[...] = jnp.zeros_like(st_ref)

    b_all = _chunk_cumsum(lf_ref[0])
    q_all = q_ref[0].astype(F32)
    k_all = k_ref[0].astype(F32)
    q_dec_all = (q_all * jnp.exp(b_all)).astype(BF16)
    k_inv_all = (k_all * jnp.exp(-b_all)).astype(BF16)
    crow = lax.broadcasted_iota(jnp.int32, (C, C), 0)
    ccol = lax.broadcasted_iota(jnp.int32, (C, C), 1)

    n_chunks = ts // C
    sls = [slice(n * C, (n + 1) * C) for n in range(n_chunks)]
    b_lasts = [b_all[(n + 1) * C - 1:(n + 1) * C, :] for n in range(n_chunks)]
    a_list, ut_list = [], []
    for n, sl in enumerate(sls):
        a_list.append(lax.dot_general(q_dec_all[sl, :], k_inv_all[sl, :], (((1,), (1,)), ((), ())),
                                      preferred_element_type=F32))
        k_state = (k_all[sl, :] * jnp.exp(b_lasts[n] - b_all[sl, :])).astype(BF16)
        ut_list.append(lax.dot_general(i_ref[0, sl, :], k_state, (((0,), (0,)), ((), ())),
                                       preferred_element_type=F32))
    st = st_ref[...]
    st_list = []
    for n in range(n_chunks):
        st_list.append(st.astype(BF16))
        st = jnp.exp(b_lasts[n]) * st + ut_list[n]
    st_ref[...] = st
    outs = []
    for n, sl in enumerate(sls):
        a = jnp.where(ccol <= crow, a_list[n], 0.0).astype(BF16)
        o = jnp.dot(a, i_ref[0, sl, :], preferred_element_type=F32)
        outs.append(o + lax.dot_general(q_dec_all[sl, :], st_list[n], (((1,), (1,)), ((), ())),
                                        preferred_element_type=F32))
    o = jnp.concatenate(outs, axis=0)
    o = o * lax.rsqrt(jnp.mean(o * o, axis=-1, keepdims=True) + RMS_EPS)
    o = o * og_ref[...] * sg_ref[0].astype(F32)
    o_ref[0] = o.astype(BF16)


def _hgrn_core(q, lf, k, iv, sg, og, *, ts):
    B, S, D = q.shape
    H = D // HGRN_HEAD_DIM
    tile = pl.BlockSpec((1, ts, HGRN_HEAD_DIM), lambda b, h, s: (b, s, h))
    return pl.pallas_call(
        _hgrn_core_kernel,
        grid=(B, H, S // ts),
        in_specs=[tile] * 5 + [pl.BlockSpec((1, HGRN_HEAD_DIM), lambda b, h, s: (0, h))],
        out_specs=tile,
        out_shape=jax.ShapeDtypeStruct((B, S, D), BF16),
        scratch_shapes=[pltpu.VMEM((HGRN_HEAD_DIM, HGRN_HEAD_DIM), F32)],
        compiler_params=pltpu.CompilerParams(
            dimension_semantics=("arbitrary", "arbitrary", "arbitrary"),
            vmem_limit_bytes=VMEM_LIMIT),
        name="hgrn_core",
    )(q, lf, k, iv, sg, og)


def kernel(x, att_norm_g, att_w_in, att_b_f, att_w_out, hgrn_norm_g, hgrn_w_in, hgrn_lb_logits,
           hgrn_onorm_g, hgrn_w_out, ffn_norm_g, ffn_w_up, ffn_conv_w, ffn_conv_b, ffn_w_down,
           final_norm_g):
    B, S, D = x.shape
    depth = ffn_norm_g.shape[0]
    assert D == FOX_HEADS * FOX_HEAD_DIM and FOX_HEADS * AUG_LANES == LANES
    assert S % ROW_TILE == 0 and S % ATTN_TILE == 0
    assert S % HGRN_TILE == 0 and HGRN_TILE % HGRN_CHUNK == 0
    assert (ffn_w_up.shape[2] // 2) % FFN_CHUNK == 0

    sm = jax.nn.softmax(hgrn_lb_logits.astype(F32), axis=0)
    lower_bounds = jnp.cumsum(sm, axis=0) - sm[0:1]
    fg = final_norm_g.reshape(1, D)
    hsum = (jnp.arange(D)[:, None] // FOX_HEAD_DIM
            == jnp.arange(LANES)[None, :] // AUG_LANES).astype(BF16)

    for layer in range(depth):
        j = layer // 2
        if layer % 2 == 0:
            w_in = att_w_in[j]
            wqkv = w_in[:, :3 * D].astype(BF16)
            wf = jnp.repeat(w_in[:, 3 * D:], AUG_LANES, axis=1).astype(BF16)
            bf = jnp.repeat(att_b_f[j], AUG_LANES).reshape(1, LANES)
            qkv, augq, augk, stats = _fox_in(x, att_norm_g[j].reshape(1, D), wqkv, wf, bf, hsum,
                                             ts=ATTN_TILE)
            o = _fox_attn(_attn_block_counts(stats), qkv, augq, augk, t=ATTN_TILE)
            w_out = att_w_out[j]
        else:
            q, lf, k, iv, sg = _hgrn_in(x, hgrn_norm_g[j].reshape(1, D), hgrn_w_in[j].astype(BF16),
                                        lower_bounds[layer].reshape(1, D), ts=ROW_TILE)
            o = _hgrn_core(q, lf, k, iv, sg, hgrn_onorm_g[j].reshape(1, D), ts=HGRN_TILE)
            w_out = hgrn_w_out[j]
        wg, wv, cg, cv, wd = _ffn_weights(ffn_w_up[layer], ffn_conv_w[layer], ffn_conv_b[layer],
                                          ffn_w_down[layer])
        x = _ffn(x, o, w_out.astype(BF16), ffn_norm_g[layer].reshape(1, D), wg, wv, cg, cv, wd, fg,
                 ts=ROW_TILE, final=(layer == depth - 1))
    return x
```

```python
import functools
import math

import jax
import jax.numpy as jnp
from jax import lax
from jax.experimental import pallas as pl
from jax.experimental.pallas import tpu as pltpu

F32 = jnp.float32
BF16 = jnp.bfloat16

RMS_EPS = 1e-6
FOX_HEADS = 16
FOX_HEAD_DIM = 64
HGRN_HEAD_DIM = 128
HGRN_CHUNK = 64
CONV_WIDTH = 3
LOG2E = math.log2(math.e)

LANES = 128
AUG_LANES = 8
NEG_BIG = -1e30
NORM_SLACK = 1.01
EXP2_ZERO_GAP = 160.0
VMEM_LIMIT = 56 * 1024 * 1024

ROW_TILE = 512
ATTN_TILE = 512
HGRN_TILE = 1024
FFN_CHUNK = 256


def _rmsnorm(x, g):
    ms = jnp.mean(x * x, axis=-1, keepdims=True)
    return x * lax.rsqrt(ms + RMS_EPS) * g


def _split3(x):
    hi = x.astype(BF16)
    r = x - hi.astype(F32)
    mid = r.astype(BF16)
    lo = (r - mid.astype(F32)).astype(BF16)
    return hi, mid, lo


def _silu(x):
    return x * (1.0 / (1.0 + jnp.exp(-x)))


def _tile_lanes(x, n):
    return jnp.concatenate([x] * n, axis=1) if n > 1 else x


def _fox_in_kernel(x_ref, g_ref, wqkv_ref, wf_ref, bf_ref, hsum_ref, qkv_ref, augq_ref, augk_ref,
                   stats_ref, carry_ref, *, d_model, q_scale):
    ts = x_ref.shape[1]

    @pl.when(pl.program_id(1) == 0)
    def _():
        carry_ref[...] = jnp.zeros_like(carry_ref)

    h = _rmsnorm(x_ref[0], g_ref[...]).astype(BF16)
    norm2_max = []
    for c in range(3):
        acc = jnp.dot(h, wqkv_ref[:, c * d_model:(c + 1) * d_model], preferred_element_type=F32)
        if c == 0:
            acc = acc * q_scale
        yb = acc.astype(BF16)
        qkv_ref[0, :, c * d_model:(c + 1) * d_model] = yb
        if c < 2:
            sq = jnp.square(yb.astype(F32)).astype(BF16)
            n2 = jnp.dot(sq, hsum_ref[...], preferred_element_type=F32) * NORM_SLACK
            norm2_max.append(jnp.max(n2, axis=0, keepdims=True))

    z = jnp.dot(h, wf_ref[...], preferred_element_type=F32) + bf_ref[...]
    logf = jnp.minimum(z, 0.0) - jnp.log1p(jnp.exp(-jnp.abs(z)))
    lc = logf * LOG2E
    row = lax.broadcasted_iota(jnp.int32, (ts, ts), 0)
    col = lax.broadcasted_iota(jnp.int32, (ts, ts), 1)
    tri = jnp.where(col <= row, 1.0, 0.0).astype(BF16)
    hi, mid, lo = _split3(lc)
    cs = (jnp.dot(tri, hi, preferred_element_type=F32)
          + jnp.dot(tri, mid, preferred_element_type=F32)
          + jnp.dot(tri, lo, preferred_element_type=F32)) + carry_ref[...]
    carry_ref[...] = cs[ts - 1:ts, :]

    c_hi, c_mid, c_lo = (t.astype(F32) for t in _split3(cs))
    j = lax.broadcasted_iota(jnp.int32, (ts, LANES), 1) & (AUG_LANES - 1)
    augq = jnp.where(j == 0, c_hi, jnp.where(j == 1, c_mid, jnp.where(j == 2, c_lo,
                     jnp.where(j < 6, 1.0, 0.0))))
    augk = jnp.where(j < 3, 1.0, jnp.where(j == 3, -c_hi, jnp.where(j == 4, -c_mid,
                     jnp.where(j == 5, -c_lo, 0.0))))
    augq_ref[0] = augq.astype(BF16)
    augk_ref[0] = augk.astype(BF16)

    stats_ref[0, 0] = jnp.concatenate(
        [norm2_max[0], norm2_max[1], cs[0:1, :], cs[ts - 1:ts, :], jnp.zeros((4, LANES), F32)],
        axis=0)


def _fox_in(x, g, wqkv, wf, bf, hsum, *, ts):
    B, S, D = x.shape
    q_scale = LOG2E / math.sqrt(FOX_HEAD_DIM)
    const = lambda b, s: (0, 0)
    return pl.pallas_call(
        functools.partial(_fox_in_kernel, d_model=D, q_scale=q_scale),
        grid=(B, S // ts),
        in_specs=[
            pl.BlockSpec((1, ts, D), lambda b, s: (b, s, 0)),
            pl.BlockSpec((1, D), const),
            pl.BlockSpec((D, 3 * D), const),
            pl.BlockSpec((D, LANES), const),
            pl.BlockSpec((1, LANES), const),
            pl.BlockSpec((D, LANES), const),
        ],
        out_specs=[
            pl.BlockSpec((1, ts, 3 * D), lambda b, s: (b, s, 0)),
            pl.BlockSpec((1, ts, LANES), lambda b, s: (b, s, 0)),
            pl.BlockSpec((1, ts, LANES), lambda b, s: (b, s, 0)),
            pl.BlockSpec((1, 1, 8, LANES), lambda b, s: (b, s, 0, 0)),
        ],
        out_shape=[
            jax.ShapeDtypeStruct((B, S, 3 * D), BF16),
            jax.ShapeDtypeStruct((B, S, LANES), BF16),
            jax.ShapeDtypeStruct((B, S, LANES), BF16),
            jax.ShapeDtypeStruct((B, S // ts, 8, LANES), F32),
        ],
        scratch_shapes=[pltpu.VMEM((1, LANES), F32)],
        compiler_params=pltpu.CompilerParams(
            dimension_semantics=("arbitrary", "arbitrary"), vmem_limit_bytes=VMEM_LIMIT),
        name="fox_in",
    )(x, g, wqkv, wf, bf, hsum)


def _attn_block_counts(stats):
    B, nt = stats.shape[0], stats.shape[1]
    per_head = stats[:, :, :4, ::AUG_LANES]
    qn2, kn2, c_first, c_last = (per_head[:, :, r] for r in range(4))
    kn2_max = jnp.max(kn2, axis=1, keepdims=True)
    bound = 2.0 * jnp.sqrt(qn2 * kn2_max) + c_first + 1.0
    gap = bound[:, :, None, :] - c_last[:, None, :, :]
    kk = lax.broadcasted_iota(jnp.int32, (nt, nt), 1)
    ii = lax.broadcasted_iota(jnp.int32, (nt, nt), 0)
    need = (gap > -EXP2_ZERO_GAP) & (kk < ii)[None, :, :, None]
    oldest = jnp.min(jnp.where(need, kk[None, :, :, None], nt), axis=2)
    count = jnp.maximum(jnp.arange(nt)[None, :, None] - oldest, 0) + 1
    pairs = jnp.max(count.reshape(B, nt, -1, 2), axis=-1)
    return jnp.transpose(pairs, (0, 2, 1)).reshape(-1).astype(jnp.int32)


def _fox_attn_kernel(nblk_ref, q_ref, augq_ref, k_ref, v_ref, augk_ref, o_ref, lhs_ref, m_ref,
                     acc_ref, sa_ref, sb_ref, pa_ref, pb_ref, ala_ref, alb_ref, mxa_ref, mxb_ref,
                     *, t):
    j = pl.program_id(1)
    i = pl.program_id(2)
    n = nblk_ref[(pl.program_id(0) * pl.num_programs(1) + j) * pl.num_programs(2) + i]
    lane = lax.broadcasted_iota(jnp.int32, (t, LANES), 1)
    q2 = q_ref[0]
    aq = augq_ref[0]
    zero = jnp.zeros((), BF16)
    for hh in range(2):
        lo = (2 * j + hh) * AUG_LANES
        qm = jnp.where((lane >= FOX_HEAD_DIM * hh) & (lane < FOX_HEAD_DIM * (hh + 1)), q2, zero)
        am = jnp.where((lane >= lo) & (lane < lo + AUG_LANES), aq, zero)
        lhs_ref[hh] = jnp.concatenate([qm, am], axis=1)
        m_ref[hh] = jnp.full((t, LANES), NEG_BIG, F32)
        acc_ref[hh] = jnp.zeros((t, LANES), F32)

    one = jnp.ones((), BF16)
    row = lax.broadcasted_iota(jnp.int32, (t, t), 0)
    col = lax.broadcasted_iota(jnp.int32, (t, t), 1)

    def scores(jb, buf, masked=False):
        s_ref, mx_ref, _, _ = buf
        off = pl.multiple_of((i - jb) * t, t)
        rhs = jnp.concatenate([k_ref[0, pl.ds(off, t), :], augk_ref[0, pl.ds(off, t), :]], axis=1)
        for hh in range(2):
            s = lax.dot_general(lhs_ref[hh], rhs, (((1,), (1,)), ((), ())),
                                preferred_element_type=F32)
            if masked:
                s = jnp.where(col <= row, s, NEG_BIG)
            s_ref[hh] = s
            mx = s[:, :LANES]
            for c in range(1, t // LANES):
                mx = jnp.maximum(mx, s[:, c * LANES:(c + 1) * LANES])
            mx_ref[hh] = mx

    def softmax(buf):
        s_ref, mx_ref, p_ref, al_ref = buf
        for hh in range(2):
            m_prev = m_ref[hh]
            m_new = jnp.maximum(m_prev, jnp.max(mx_ref[hh], axis=1, keepdims=True))
            al_ref[hh] = jnp.exp2(m_prev - m_new)
            p_ref[hh] = jnp.exp2(s_ref[hh] - _tile_lanes(m_new, t // LANES)).astype(BF16)
            m_ref[hh] = m_new

    def values(jb, buf):
        _, _, p_ref, al_ref = buf
        off = pl.multiple_of((i - jb) * t, t)
        v2 = v_ref[0, pl.ds(off, t), :]
        for hh in range(2):
            own = (lane >= FOX_HEAD_DIM * hh) & (lane < FOX_HEAD_DIM * (hh + 1))
            pv = jnp.dot(p_ref[hh], jnp.where(own, v2, one), preferred_element_type=F32)
            acc_ref[hh] = al_ref[hh] * acc_ref[hh] + pv

    n_steady = jnp.maximum(n - 2, 0)

    buf_a = (sa_ref, mxa_ref, pa_ref, ala_ref)
    buf_b = (sb_ref, mxb_ref, pb_ref, alb_ref)

    scores(0, buf_a, masked=True)

    @pl.when(n > 1)
    def _():
        softmax(buf_a)
        scores(1, buf_b)

    @pl.when(n == 1)
    def _():
        softmax(buf_a)

    def step_pair(u, carry):
        tau = 2 + 2 * u
        values(tau - 2, buf_a)
        softmax(buf_b)
        scores(tau, buf_a)
        values(tau - 1, buf_b)
        softmax(buf_a)
        scores(tau + 1, buf_b)
        return carry

    lax.fori_loop(0, n_steady // 2, step_pair, 0)

    @pl.when(n_steady % 2 == 1)
    def _():
        values(n - 3, buf_a)
        softmax(buf_b)
        scores(n - 1, buf_a)

    @pl.when(n % 2 == 0)
    def _():
        values(n - 2, buf_a)
        softmax(buf_b)
        values(n - 1, buf_b)

    @pl.when((n % 2 == 1) & (n >= 3))
    def _():
        values(n - 2, buf_b)
        softmax(buf_a)
        values(n - 1, buf_a)

    @pl.when(n == 1)
    def _():
        values(0, buf_a)

    out_a = acc_ref[0] / pltpu.roll(acc_ref[0], FOX_HEAD_DIM, axis=1)
    out_b = acc_ref[1] / pltpu.roll(acc_ref[1], FOX_HEAD_DIM, axis=1)
    o_ref[0] = jnp.where(lane < FOX_HEAD_DIM, out_a, out_b).astype(BF16)


def _fox_attn(nblk, qkv, augq, augk, *, t):
    B, S, D3 = qkv.shape
    D = D3 // 3
    nb = D // LANES
    grid_spec = pltpu.PrefetchScalarGridSpec(
        num_scalar_prefetch=1,
        grid=(B, nb, S // t),
        in_specs=[
            pl.BlockSpec((1, t, LANES), lambda b, j, i, nblk: (b, i, j)),
            pl.BlockSpec((1, t, LANES), lambda b, j, i, nblk: (b, i, 0)),
            pl.BlockSpec((1, S, LANES), lambda b, j, i, nblk: (b, 0, nb + j)),
            pl.BlockSpec((1, S, LANES), lambda b, j, i, nblk: (b, 0, 2 * nb + j)),
            pl.BlockSpec((1, S, LANES), lambda b, j, i, nblk: (b, 0, 0)),
        ],
        out_specs=pl.BlockSpec((1, t, LANES), lambda b, j, i, nblk: (b, i, j)),
        scratch_shapes=[
            pltpu.VMEM((2, t, 2 * LANES), BF16),
            pltpu.VMEM((2, t, LANES), F32),
            pltpu.VMEM((2, t, LANES), F32),
            pltpu.VMEM((2, t, t), F32),
            pltpu.VMEM((2, t, t), F32),
            pltpu.VMEM((2, t, t), BF16),
            pltpu.VMEM((2, t, t), BF16),
            pltpu.VMEM((2, t, LANES), F32),
            pltpu.VMEM((2, t, LANES), F32),
            pltpu.VMEM((2, t, LANES), F32),
            pltpu.VMEM((2, t, LANES), F32),
        ],
    )
    return pl.pallas_call(
        functools.partial(_fox_attn_kernel, t=t),
        grid_spec=grid_spec,
        out_shape=jax.ShapeDtypeStruct((B, S, D), BF16),
        compiler_params=pltpu.CompilerParams(
            dimension_semantics=("arbitrary", "arbitrary", "arbitrary"),
            vmem_limit_bytes=VMEM_LIMIT),
        name="fox_attn",
    )(nblk, qkv, augq, qkv, qkv, augk)


def _ffn_kernel(x_ref, o_ref, wout_ref, g_ref, wg_ref, wv_ref, cg_ref, cv_ref, wd_ref, fg_ref,
                out_ref, tail_ref, x1_ref, h_ref, acc_ref, u_ref, z_ref, *, final):
    ts = x_ref.shape[1]
    n_chunks = wg_ref.shape[0]

    @pl.when(pl.program_id(1) == 0)
    def _():
        tail_ref[...] = jnp.zeros_like(tail_ref)

    x1 = x_ref[0] + jnp.dot(o_ref[0], wout_ref[...], preferred_element_type=F32)
    x1_ref[...] = x1
    h_ref[...] = _rmsnorm(x1, g_ref[...]).astype(BF16)
    acc_ref[...] = jnp.zeros_like(acc_ref)
    def up(c):
        h = h_ref[...]
        for stream, w_ref in enumerate((wg_ref, wv_ref)):
            u = jnp.dot(h, w_ref[c], preferred_element_type=F32)
            u_ref[c % 2, stream, 0:8, :] = tail_ref[stream, c]
            u_ref[c % 2, stream, 8:, :] = u
            tail_ref[stream, c] = u[ts - 8:, :]

    def conv(c, stream, cw):
        u0 = u_ref[c % 2, stream, 8:8 + ts, :]
        u1 = u_ref[c % 2, stream, 7:7 + ts, :]
        u2 = u_ref[c % 2, stream, 6:6 + ts, :]
        return cw[3:4, :] + cw[0:1, :] * u2 + cw[1:2, :] * u1 + cw[2:3, :] * u0

    def act(c):
        gate = conv(c, 0, cg_ref[c])
        val = conv(c, 1, cv_ref[c])
        z_ref[c % 2] = (_silu(gate) * val).astype(BF16)

    def down(c):
        acc_ref[...] += jnp.dot(z_ref[c % 2], wd_ref[c], preferred_element_type=F32)

    up(0)
    if n_chunks > 1:
        up(1)
    act(0)
    for c in range(n_chunks):
        if c + 2 < n_chunks:
            up(c + 2)
        if c + 1 < n_chunks:
            act(c + 1)
        down(c)
    y = x1_ref[...] + acc_ref[...]
    if final:
        y = _rmsnorm(y, fg_ref[...])
    out_ref[0] = y


def _ffn(x, o, wout, g, wg, wv, cg, cv, wd, fg, *, ts, final):
    B, S, D = x.shape
    n_chunks = wg.shape[0]
    c2 = lambda b, s: (0, 0)
    c3 = lambda b, s: (0, 0, 0)
    return pl.pallas_call(
        functools.partial(_ffn_kernel, final=final),
        grid=(B, S // ts),
        in_specs=[
            pl.BlockSpec((1, ts, D), lambda b, s: (b, s, 0)),
            pl.BlockSpec((1, ts, D), lambda b, s: (b, s, 0)),
            pl.BlockSpec((D, D), c2),
            pl.BlockSpec((1, D), c2),
            pl.BlockSpec(wg.shape, c3),
            pl.BlockSpec(wv.shape, c3),
            pl.BlockSpec(cg.shape, c3),
            pl.BlockSpec(cv.shape, c3),
            pl.BlockSpec(wd.shape, c3),
            pl.BlockSpec((1, D), c2),
        ],
        out_specs=pl.BlockSpec((1, ts, D), lambda b, s: (b, s, 0)),
        out_shape=jax.ShapeDtypeStruct((B, S, D), F32),
        scratch_shapes=[
            pltpu.VMEM((2, n_chunks, 8, FFN_CHUNK), F32),
            pltpu.VMEM((ts, D), F32),
            pltpu.VMEM((ts, D), BF16),
            pltpu.VMEM((ts, D), F32),
            pltpu.VMEM((2, 2, 8 + ts, FFN_CHUNK), F32),
            pltpu.VMEM((2, ts, FFN_CHUNK), BF16),
        ],
        compiler_params=pltpu.CompilerParams(
            dimension_semantics=("arbitrary", "arbitrary"), vmem_limit_bytes=VMEM_LIMIT),
        name="ffn_final" if final else "ffn",
    )(x, o, wout, g, wg, wv, cg, cv, wd, fg)


def _ffn_weights(w_up, conv_w, conv_b, w_down):
    D, F2 = w_up.shape
    F = F2 // 2
    n = F // FFN_CHUNK

    def cols(w):
        return w.reshape(D, n, FFN_CHUNK).transpose(1, 0, 2).astype(BF16)

    def taps(w, b):
        t = jnp.concatenate([w, b[None, :], jnp.zeros((8 - CONV_WIDTH - 1, F), F32)], axis=0)
        return t.reshape(8, n, FFN_CHUNK).transpose(1, 0, 2)

    return (cols(w_up[:, :F]), cols(w_up[:, F:]),
            taps(conv_w[:, :F], conv_b[:F]), taps(conv_w[:, F:], conv_b[F:]),
            w_down.reshape(n, FFN_CHUNK, D).astype(BF16))


def _hgrn_in_kernel(x_ref, g_ref, w_ref, lb_ref, q_ref, lf_ref, k_ref, i_ref, sg_ref, *, d_model):
    h = _rmsnorm(x_ref[0], g_ref[...]).astype(BF16)
    D = d_model
    q = jnp.dot(h, w_ref[:, 0:D], preferred_element_type=F32)
    q_ref[0] = _silu(q).astype(BF16)
    fl = jnp.dot(h, w_ref[:, D:2 * D], preferred_element_type=F32)
    lb = lb_ref[...]
    f = lb + (1.0 - lb) * (1.0 / (1.0 + jnp.exp(-fl)))
    lf_ref[0] = jnp.log(f)
    k_ref[0] = (1.0 - f).astype(BF16)
    i_ref[0] = jnp.dot(h, w_ref[:, 2 * D:3 * D], preferred_element_type=F32).astype(BF16)
    gg = jnp.dot(h, w_ref[:, 3 * D:4 * D], preferred_element_type=F32)
    sg_ref[0] = _silu(gg).astype(BF16)


def _hgrn_in(x, g, w, lb, *, ts):
    B, S, D = x.shape
    const = lambda b, s: (0, 0)
    tile = pl.BlockSpec((1, ts, D), lambda b, s: (b, s, 0))
    return pl.pallas_call(
        functools.partial(_hgrn_in_kernel, d_model=D),
        grid=(B, S // ts),
        in_specs=[tile, pl.BlockSpec((1, D), const), pl.BlockSpec((D, 4 * D), const),
                  pl.BlockSpec((1, D), const)],
        out_specs=[tile] * 5,
        out_shape=[
            jax.ShapeDtypeStruct((B, S, D), BF16),
            jax.ShapeDtypeStruct((B, S, D), F32),
            jax.ShapeDtypeStruct((B, S, D), BF16),
            jax.ShapeDtypeStruct((B, S, D), BF16),
            jax.ShapeDtypeStruct((B, S, D), BF16),
        ],
        compiler_params=pltpu.CompilerParams(
            dimension_semantics=("arbitrary", "arbitrary"), vmem_limit_bytes=VMEM_LIMIT),
        name="hgrn_in",
    )(x, g, w, lb)


def _chunk_cumsum(x):
    r = lax.broadcasted_iota(jnp.int32, x.shape, 0) & (HGRN_CHUNK - 1)
    k = 1
    while k < HGRN_CHUNK:
        x = x + jnp.where(r >= k, pltpu.roll(x, k, axis=0), 0.0)
        k *= 2
    return x


def _hgrn_core_kernel(q_ref, lf_ref, k_ref, i_ref, sg_ref, og_ref, o_ref, st_ref):
    ts = q_ref.shape[1]
    C = HGRN_CHUNK

    @pl.when(pl.program_id(2) == 0)
    def _():
        st_ref[...] = jnp.zeros_like(st_ref)

    b_all = _chunk_cumsum(lf_ref[0])
    q_all = q_ref[0].astype(F32)
    k_all = k_ref[0].astype(F32)
    q_dec_all = (q_all * jnp.exp(b_all)).astype(BF16)
    k_inv_all = (k_all * jnp.exp(-b_all)).astype(BF16)
    crow = lax.broadcasted_iota(jnp.int32, (C, C), 0)
    ccol = lax.broadcasted_iota(jnp.int32, (C, C), 1)

    n_chunks = ts // C
    sls = [slice(n * C, (n + 1) * C) for n in range(n_chunks)]
    b_lasts = [b_all[(n + 1) * C - 1:(n + 1) * C, :] for n in range(n_chunks)]
    a_list, ut_list = [], []
    for n, sl in enumerate(sls):
        a_list.append(lax.dot_general(q_dec_all[sl, :], k_inv_all[sl, :], (((1,), (1,)), ((), ())),
                                      preferred_element_type=F32))
        k_state = (k_all[sl, :] * jnp.exp(b_lasts[n] - b_all[sl, :])).astype(BF16)
        ut_list.append(lax.dot_general(i_ref[0, sl, :], k_state, (((0,), (0,)), ((), ())),
                                       preferred_element_type=F32))
    st = st_ref[...]
    st_list = []
    for n in range(n_chunks):
        st_list.append(st.astype(BF16))
        st = jnp.exp(b_lasts[n]) * st + ut_list[n]
    st_ref[...] = st
    outs = []
    for n, sl in enumerate(sls):
        a = jnp.where(ccol <= crow, a_list[n], 0.0).astype(BF16)
        o = jnp.dot(a, i_ref[0, sl, :], preferred_element_type=F32)
        outs.append(o + lax.dot_general(q_dec_all[sl, :], st_list[n], (((1,), (1,)), ((), ())),
                                        preferred_element_type=F32))
    o = jnp.concatenate(outs, axis=0)
    o = o * lax.rsqrt(jnp.mean(o * o, axis=-1, keepdims=True) + RMS_EPS)
    o = o * og_ref[...] * sg_ref[0].astype(F32)
    o_ref[0] = o.astype(BF16)


def _hgrn_core(q, lf, k, iv, sg, og, *, ts):
    B, S, D = q.shape
    H = D // HGRN_HEAD_DIM
    tile = pl.BlockSpec((1, ts, HGRN_HEAD_DIM), lambda b, h, s: (b, s, h))
    return pl.pallas_call(
        _hgrn_core_kernel,
        grid=(B, H, S // ts),
        in_specs=[tile] * 5 + [pl.BlockSpec((1, HGRN_HEAD_DIM), lambda b, h, s: (0, h))],
        out_specs=tile,
        out_shape=jax.ShapeDtypeStruct((B, S, D), BF16),
        scratch_shapes=[pltpu.VMEM((HGRN_HEAD_DIM, HGRN_HEAD_DIM), F32)],
        compiler_params=pltpu.CompilerParams(
            dimension_semantics=("arbitrary", "arbitrary", "arbitrary"),
            vmem_limit_bytes=VMEM_LIMIT),
        name="hgrn_core",
    )(q, lf, k, iv, sg, og)


def kernel(x, att_norm_g, att_w_in, att_b_f, att_w_out, hgrn_norm_g, hgrn_w_in, hgrn_lb_logits,
           hgrn_onorm_g, hgrn_w_out, ffn_norm_g, ffn_w_up, ffn_conv_w, ffn_conv_b, ffn_w_down,
           final_norm_g):
    B, S, D = x.shape
    depth = ffn_norm_g.shape[0]
    assert D == FOX_HEADS * FOX_HEAD_DIM and FOX_HEADS * AUG_LANES == LANES
    assert S % ROW_TILE == 0 and S % ATTN_TILE == 0
    assert S % HGRN_TILE == 0 and HGRN_TILE % HGRN_CHUNK == 0
    assert (ffn_w_up.shape[2] // 2) % FFN_CHUNK == 0

    sm = jax.nn.softmax(hgrn_lb_logits.astype(F32), axis=0)
    lower_bounds = jnp.cumsum(sm, axis=0) - sm[0:1]
    fg = final_norm_g.reshape(1, D)
    hsum = (jnp.arange(D)[:, None] // FOX_HEAD_DIM
            == jnp.arange(LANES)[None, :] // AUG_LANES).astype(BF16)

    for layer in range(depth):
        j = layer // 2
        if layer % 2 == 0:
            order = jnp.argsort(att_b_f[j])
            cols = (order[:, None] * FOX_HEAD_DIM + jnp.arange(FOX_HEAD_DIM)[None, :]).reshape(-1)
            w_in = att_w_in[j]
            wqkv = jnp.concatenate([jnp.take(w_in[:, s * D:(s + 1) * D], cols, axis=1)
                                    for s in range(3)], axis=1).astype(BF16)
            wf = jnp.repeat(jnp.take(w_in[:, 3 * D:], order, axis=1), AUG_LANES,
                            axis=1).astype(BF16)
            bf = jnp.repeat(jnp.take(att_b_f[j], order), AUG_LANES).reshape(1, LANES)
            qkv, augq, augk, stats = _fox_in(x, att_norm_g[j].reshape(1, D), wqkv, wf, bf, hsum,
                                             ts=ATTN_TILE)
            o = _fox_attn(_attn_block_counts(stats), qkv, augq, augk, t=ATTN_TILE)
            w_out = jnp.take(att_w_out[j], cols, axis=0)
        else:
            q, lf, k, iv, sg = _hgrn_in(x, hgrn_norm_g[j].reshape(1, D), hgrn_w_in[j].astype(BF16),
                                        lower_bounds[layer].reshape(1, D), ts=ROW_TILE)
            o = _hgrn_core(q, lf, k, iv, sg, hgrn_onorm_g[j].reshape(1, D), ts=HGRN_TILE)
            w_out = hgrn_w_out[j]
        wg, wv, cg, cv, wd = _ffn_weights(ffn_w_up[layer], ffn_conv_w[layer], ffn_conv_b[layer],
                                          ffn_w_down[layer])
        x = _ffn(x, o, w_out.astype(BF16), ffn_norm_g[layer].reshape(1, D), wg, wv, cg, cv, wd, fg,
                 ts=ROW_TILE, final=(layer == depth - 1))
    return x
```

```python
import functools
import math

import jax
import jax.numpy as jnp
from jax import lax
from jax.experimental import pallas as pl
from jax.experimental.pallas import tpu as pltpu

F32 = jnp.float32
BF16 = jnp.bfloat16

RMS_EPS = 1e-6
FOX_HEADS = 16
FOX_HEAD_DIM = 64
HGRN_HEAD_DIM = 128
HGRN_CHUNK = 64
CONV_WIDTH = 3
LOG2E = math.log2(math.e)

LANES = 128
AUG_LANES = 8
NEG_BIG = -1e30
NORM_SLACK = 1.01
EXP2_ZERO_GAP = 160.0
VMEM_LIMIT = 56 * 1024 * 1024

ROW_TILE = 512
ATTN_TILE = 512
HGRN_TILE = 1024
FFN_CHUNK = 256
FFN_STAGES = 3


def _rmsnorm(x, g):
    ms = jnp.mean(x * x, axis=-1, keepdims=True)
    return x * lax.rsqrt(ms + RMS_EPS) * g


def _split3(x):
    hi = x.astype(BF16)
    r = x - hi.astype(F32)
    mid = r.astype(BF16)
    lo = (r - mid.astype(F32)).astype(BF16)
    return hi, mid, lo


def _silu(x):
    return x * (1.0 / (1.0 + jnp.exp(-x)))


def _tile_lanes(x, n):
    return jnp.concatenate([x] * n, axis=1) if n > 1 else x


def _fox_in_kernel(x_ref, g_ref, wqkv_ref, wf_ref, bf_ref, hsum_ref, qkv_ref, augq_ref, augk_ref,
                   stats_ref, carry_ref, *, d_model, q_scale):
    ts = x_ref.shape[1]

    @pl.when(pl.program_id(1) == 0)
    def _():
        carry_ref[...] = jnp.zeros_like(carry_ref)

    h = _rmsnorm(x_ref[0], g_ref[...]).astype(BF16)
    norm2_max = []
    for c in range(3):
        acc = jnp.dot(h, wqkv_ref[:, c * d_model:(c + 1) * d_model], preferred_element_type=F32)
        if c == 0:
            acc = acc * q_scale
        yb = acc.astype(BF16)
        qkv_ref[0, :, c * d_model:(c + 1) * d_model] = yb
        if c < 2:
            sq = jnp.square(yb.astype(F32)).astype(BF16)
            n2 = jnp.dot(sq, hsum_ref[...], preferred_element_type=F32) * NORM_SLACK
            norm2_max.append(jnp.max(n2, axis=0, keepdims=True))

    z = jnp.dot(h, wf_ref[...], preferred_element_type=F32) + bf_ref[...]
    logf = jnp.minimum(z, 0.0) - jnp.log1p(jnp.exp(-jnp.abs(z)))
    lc = logf * LOG2E
    row = lax.broadcasted_iota(jnp.int32, (ts, ts), 0)
    col = lax.broadcasted_iota(jnp.int32, (ts, ts), 1)
    tri = jnp.where(col <= row, 1.0, 0.0).astype(BF16)
    hi, mid, lo = _split3(lc)
    cs = (jnp.dot(tri, hi, preferred_element_type=F32)
          + jnp.dot(tri, mid, preferred_element_type=F32)
          + jnp.dot(tri, lo, preferred_element_type=F32)) + carry_ref[...]
    carry_ref[...] = cs[ts - 1:ts, :]

    c_hi, c_mid, c_lo = (t.astype(F32) for t in _split3(cs))
    j = lax.broadcasted_iota(jnp.int32, (ts, LANES), 1) & (AUG_LANES - 1)
    augq = jnp.where(j == 0, c_hi, jnp.where(j == 1, c_mid, jnp.where(j == 2, c_lo,
                     jnp.where(j < 6, 1.0, 0.0))))
    augk = jnp.where(j < 3, 1.0, jnp.where(j == 3, -c_hi, jnp.where(j == 4, -c_mid,
                     jnp.where(j == 5, -c_lo, 0.0))))
    augq_ref[0] = augq.astype(BF16)
    augk_ref[0] = augk.astype(BF16)

    stats_ref[0, 0] = jnp.concatenate(
        [norm2_max[0], norm2_max[1], cs[0:1, :], cs[ts - 1:ts, :], jnp.zeros((4, LANES), F32)],
        axis=0)


def _fox_in(x, g, wqkv, wf, bf, hsum, *, ts):
    B, S, D = x.shape
    q_scale = LOG2E / math.sqrt(FOX_HEAD_DIM)
    const = lambda b, s: (0, 0)
    return pl.pallas_call(
        functools.partial(_fox_in_kernel, d_model=D, q_scale=q_scale),
        grid=(B, S // ts),
        in_specs=[
            pl.BlockSpec((1, ts, D), lambda b, s: (b, s, 0)),
            pl.BlockSpec((1, D), const),
            pl.BlockSpec((D, 3 * D), const),
            pl.BlockSpec((D, LANES), const),
            pl.BlockSpec((1, LANES), const),
            pl.BlockSpec((D, LANES), const),
        ],
        out_specs=[
            pl.BlockSpec((1, ts, 3 * D), lambda b, s: (b, s, 0)),
            pl.BlockSpec((1, ts, LANES), lambda b, s: (b, s, 0)),
            pl.BlockSpec((1, ts, LANES), lambda b, s: (b, s, 0)),
            pl.BlockSpec((1, 1, 8, LANES), lambda b, s: (b, s, 0, 0)),
        ],
        out_shape=[
            jax.ShapeDtypeStruct((B, S, 3 * D), BF16),
            jax.ShapeDtypeStruct((B, S, LANES), BF16),
            jax.ShapeDtypeStruct((B, S, LANES), BF16),
            jax.ShapeDtypeStruct((B, S // ts, 8, LANES), F32),
        ],
        scratch_shapes=[pltpu.VMEM((1, LANES), F32)],
        compiler_params=pltpu.CompilerParams(
            dimension_semantics=("arbitrary", "arbitrary"), vmem_limit_bytes=VMEM_LIMIT),
        name="fox_in",
    )(x, g, wqkv, wf, bf, hsum)


def _attn_block_counts(stats):
    B, nt = stats.shape[0], stats.shape[1]
    per_head = stats[:, :, :4, ::AUG_LANES]
    qn2, kn2, c_first, c_last = (per_head[:, :, r] for r in range(4))
    kn2_max = jnp.max(kn2, axis=1, keepdims=True)
    bound = 2.0 * jnp.sqrt(qn2 * kn2_max) + c_first + 1.0
    gap = bound[:, :, None, :] - c_last[:, None, :, :]
    kk = lax.broadcasted_iota(jnp.int32, (nt, nt), 1)
    ii = lax.broadcasted_iota(jnp.int32, (nt, nt), 0)
    need = (gap > -EXP2_ZERO_GAP) & (kk < ii)[None, :, :, None]
    oldest = jnp.min(jnp.where(need, kk[None, :, :, None], nt), axis=2)
    count = jnp.maximum(jnp.arange(nt)[None, :, None] - oldest, 0) + 1
    pairs = jnp.max(count.reshape(B, nt, -1, 2), axis=-1)
    return jnp.transpose(pairs, (0, 2, 1)).reshape(-1).astype(jnp.int32)


def _fox_attn_kernel(nblk_ref, q_ref, *refs, t):
    j = pl.program_id(1)
    nq = q_ref.shape[1] // t

    def tile(i, carry):
        n = nblk_ref[(pl.program_id(0) * pl.num_programs(1) + j) * nq + i]
        _fox_attn_tile(i, n, j, q_ref, *refs, t=t)
        return carry

    lax.fori_loop(0, nq, tile, 0)


def _fox_attn_tile(i, n, j, q_ref, augq_ref, k_ref, v_ref, augk_ref, o_ref, lhs_ref, m_ref,
                   acc_ref, sa_ref, sb_ref, pa_ref, pb_ref, ala_ref, alb_ref, mxa_ref, mxb_ref,
                   *, t):
    lane = lax.broadcasted_iota(jnp.int32, (t, LANES), 1)
    q_rows = pl.ds(pl.multiple_of(i * t, t), t)
    q2 = q_ref[0, q_rows, :]
    aq = augq_ref[0, q_rows, :]
    zero = jnp.zeros((), BF16)
    for hh in range(2):
        lo = (2 * j + hh) * AUG_LANES
        qm = jnp.where((lane >= FOX_HEAD_DIM * hh) & (lane < FOX_HEAD_DIM * (hh + 1)), q2, zero)
        am = jnp.where((lane >= lo) & (lane < lo + AUG_LANES), aq, zero)
        lhs_ref[hh] = jnp.concatenate([qm, am], axis=1)
        m_ref[hh] = jnp.full((t, LANES), NEG_BIG, F32)
        acc_ref[hh] = jnp.zeros((t, LANES), F32)

    one = jnp.ones((), BF16)
    row = lax.broadcasted_iota(jnp.int32, (t, t), 0)
    col = lax.broadcasted_iota(jnp.int32, (t, t), 1)

    def scores(jb, buf, masked=False):
        s_ref, mx_ref, _, _ = buf
        off = pl.multiple_of((i - jb) * t, t)
        rhs = jnp.concatenate([k_ref[0, pl.ds(off, t), :], augk_ref[0, pl.ds(off, t), :]], axis=1)
        for hh in range(2):
            s = lax.dot_general(lhs_ref[hh], rhs, (((1,), (1,)), ((), ())),
                                preferred_element_type=F32)
            if masked:
                s = jnp.where(col <= row, s, NEG_BIG)
            s_ref[hh] = s
            mx = s[:, :LANES]
            for c in range(1, t // LANES):
                mx = jnp.maximum(mx, s[:, c * LANES:(c + 1) * LANES])
            mx_ref[hh] = mx

    def softmax(buf):
        s_ref, mx_ref, p_ref, al_ref = buf
        for hh in range(2):
            m_prev = m_ref[hh]
            m_new = jnp.maximum(m_prev, jnp.max(mx_ref[hh], axis=1, keepdims=True))
            al_ref[hh] = jnp.exp2(m_prev - m_new)
            p_ref[hh] = jnp.exp2(s_ref[hh] - _tile_lanes(m_new, t // LANES)).astype(BF16)
            m_ref[hh] = m_new

    def values(jb, buf):
        _, _, p_ref, al_ref = buf
        off = pl.multiple_of((i - jb) * t, t)
        v2 = v_ref[0, pl.ds(off, t), :]
        for hh in range(2):
            own = (lane >= FOX_HEAD_DIM * hh) & (lane < FOX_HEAD_DIM * (hh + 1))
            pv = jnp.dot(p_ref[hh], jnp.where(own, v2, one), preferred_element_type=F32)
            acc_ref[hh] = al_ref[hh] * acc_ref[hh] + pv

    n_steady = jnp.maximum(n - 2, 0)

    buf_a = (sa_ref, mxa_ref, pa_ref, ala_ref)
    buf_b = (sb_ref, mxb_ref, pb_ref, alb_ref)

    @pl.when(n == 1)
    def _():
        scores(0, buf_a, masked=True)
        softmax(buf_a)
        values(0, buf_a)

    @pl.when(n > 1)
    def _():
        scores(0, buf_a, masked=True)
        softmax(buf_a)
        scores(1, buf_b)

    def step_pair(u, carry):
        tau = 2 + 2 * u
        values(tau - 2, buf_a)
        softmax(buf_b)
        scores(tau, buf_a)
        values(tau - 1, buf_b)
        softmax(buf_a)
        scores(tau + 1, buf_b)
        return carry

    lax.fori_loop(0, n_steady // 2, step_pair, 0)

    @pl.when(n % 2 == 0)
    def _():
        values(n - 2, buf_a)
        softmax(buf_b)
        values(n - 1, buf_b)

    @pl.when((n % 2 == 1) & (n >= 3))
    def _():
        values(n - 3, buf_a)
        softmax(buf_b)
        scores(n - 1, buf_a)
        values(n - 2, buf_b)
        softmax(buf_a)
        values(n - 1, buf_a)

    out_a = acc_ref[0] / pltpu.roll(acc_ref[0], FOX_HEAD_DIM, axis=1)
    out_b = acc_ref[1] / pltpu.roll(acc_ref[1], FOX_HEAD_DIM, axis=1)
    o_ref[0, q_rows, :] = jnp.where(lane < FOX_HEAD_DIM, out_a, out_b).astype(BF16)


def _fox_attn(nblk, qkv, augq, augk, *, t):
    B, S, D3 = qkv.shape
    D = D3 // 3
    nb = D // LANES
    grid_spec = pltpu.PrefetchScalarGridSpec(
        num_scalar_prefetch=1,
        grid=(B, nb),
        in_specs=[
            pl.BlockSpec((1, S, LANES), lambda b, j, nblk: (b, 0, j)),
            pl.BlockSpec((1, S, LANES), lambda b, j, nblk: (b, 0, 0)),
            pl.BlockSpec((1, S, LANES), lambda b, j, nblk: (b, 0, nb + j)),
            pl.BlockSpec((1, S, LANES), lambda b, j, nblk: (b, 0, 2 * nb + j)),
            pl.BlockSpec((1, S, LANES), lambda b, j, nblk: (b, 0, 0)),
        ],
        out_specs=pl.BlockSpec((1, S, LANES), lambda b, j, nblk: (b, 0, j)),
        scratch_shapes=[
            pltpu.VMEM((2, t, 2 * LANES), BF16),
            pltpu.VMEM((2, t, LANES), F32),
            pltpu.VMEM((2, t, LANES), F32),
            pltpu.VMEM((2, t, t), F32),
            pltpu.VMEM((2, t, t), F32),
            pltpu.VMEM((2, t, t), BF16),
            pltpu.VMEM((2, t, t), BF16),
            pltpu.VMEM((2, t, LANES), F32),
            pltpu.VMEM((2, t, LANES), F32),
            pltpu.VMEM((2, t, LANES), F32),
            pltpu.VMEM((2, t, LANES), F32),
        ],
    )
    return pl.pallas_call(
        functools.partial(_fox_attn_kernel, t=t),
        grid_spec=grid_spec,
        out_shape=jax.ShapeDtypeStruct((B, S, D), BF16),
        compiler_params=pltpu.CompilerParams(
            dimension_semantics=("arbitrary", "arbitrary"), vmem_limit_bytes=VMEM_LIMIT),
        name="fox_attn",
    )(nblk, qkv, augq, qkv, qkv, augk)


def _ffn_kernel(x_ref, o_ref, wout_ref, g_ref, wg_ref, wv_ref, cg_ref, cv_ref, wd_ref, fg_ref,
                out_ref, tail_ref, x1_ref, h_ref, acc_ref, u_ref, z_ref, *, final):
    ts = x_ref.shape[1]
    n_chunks = wg_ref.shape[0]

    @pl.when(pl.program_id(1) == 0)
    def _():
        tail_ref[...] = jnp.zeros_like(tail_ref)

    x1 = x_ref[0] + jnp.dot(o_ref[0], wout_ref[...], preferred_element_type=F32)
    x1_ref[...] = x1
    h_ref[...] = _rmsnorm(x1, g_ref[...]).astype(BF16)
    acc_ref[...] = jnp.zeros_like(acc_ref)
    def up(c):
        h = h_ref[...]
        for stream, w_ref in enumerate((wg_ref, wv_ref)):
            u = jnp.dot(h, w_ref[c], preferred_element_type=F32)
            u_ref[c % FFN_STAGES, stream, 0:8, :] = tail_ref[stream, c]
            u_ref[c % FFN_STAGES, stream, 8:, :] = u
            tail_ref[stream, c] = u[ts - 8:, :]

    def conv(c, stream, cw):
        u0 = u_ref[c % FFN_STAGES, stream, 8:8 + ts, :]
        u1 = u_ref[c % FFN_STAGES, stream, 7:7 + ts, :]
        u2 = u_ref[c % FFN_STAGES, stream, 6:6 + ts, :]
        return cw[3:4, :] + cw[0:1, :] * u2 + cw[1:2, :] * u1 + cw[2:3, :] * u0

    def act(c):
        gate = conv(c, 0, cg_ref[c])
        val = conv(c, 1, cv_ref[c])
        z_ref[c % FFN_STAGES] = (_silu(gate) * val).astype(BF16)

    def down(c):
        acc_ref[...] += jnp.dot(z_ref[c % FFN_STAGES], wd_ref[c], preferred_element_type=F32)

    up(0)
    if n_chunks > 1:
        up(1)
    act(0)
    for c in range(n_chunks):
        if c + 2 < n_chunks:
            up(c + 2)
        if c + 1 < n_chunks:
            act(c + 1)
        down(c)
    y = x1_ref[...] + acc_ref[...]
    if final:
        y = _rmsnorm(y, fg_ref[...])
    out_ref[0] = y


def _ffn(x, o, wout, g, wg, wv, cg, cv, wd, fg, *, ts, final):
    B, S, D = x.shape
    n_chunks = wg.shape[0]
    c2 = lambda b, s: (0, 0)
    c3 = lambda b, s: (0, 0, 0)
    return pl.pallas_call(
        functools.partial(_ffn_kernel, final=final),
        grid=(B, S // ts),
        in_specs=[
            pl.BlockSpec((1, ts, D), lambda b, s: (b, s, 0)),
            pl.BlockSpec((1, ts, D), lambda b, s: (b, s, 0)),
            pl.BlockSpec((D, D), c2),
            pl.BlockSpec((1, D), c2),
            pl.BlockSpec(wg.shape, c3),
            pl.BlockSpec(wv.shape, c3),
            pl.BlockSpec(cg.shape, c3),
            pl.BlockSpec(cv.shape, c3),
            pl.BlockSpec(wd.shape, c3),
            pl.BlockSpec((1, D), c2),
        ],
        out_specs=pl.BlockSpec((1, ts, D), lambda b, s: (b, s, 0)),
        out_shape=jax.ShapeDtypeStruct((B, S, D), F32),
        scratch_shapes=[
            pltpu.VMEM((2, n_chunks, 8, FFN_CHUNK), F32),
            pltpu.VMEM((ts, D), F32),
            pltpu.VMEM((ts, D), BF16),
            pltpu.VMEM((ts, D), F32),
            pltpu.VMEM((FFN_STAGES, 2, 8 + ts, FFN_CHUNK), F32),
            pltpu.VMEM((FFN_STAGES, ts, FFN_CHUNK), BF16),
        ],
        compiler_params=pltpu.CompilerParams(
            dimension_semantics=("arbitrary", "arbitrary"), vmem_limit_bytes=VMEM_LIMIT),
        name="ffn_final" if final else "ffn",
    )(x, o, wout, g, wg, wv, cg, cv, wd, fg)


def _ffn_weights(w_up, conv_w, conv_b, w_down):
    D, F2 = w_up.shape
    F = F2 // 2
    n = F // FFN_CHUNK

    def cols(w):
        return w.reshape(D, n, FFN_CHUNK).transpose(1, 0, 2).astype(BF16)

    def taps(w, b):
        t = jnp.concatenate([w, b[None, :], jnp.zeros((8 - CONV_WIDTH - 1, F), F32)], axis=0)
        return t.reshape(8, n, FFN_CHUNK).transpose(1, 0, 2)

    return (cols(w_up[:, :F]), cols(w_up[:, F:]),
            taps(conv_w[:, :F], conv_b[:F]), taps(conv_w[:, F:], conv_b[F:]),
            w_down.reshape(n, FFN_CHUNK, D).astype(BF16))


def _hgrn_in_kernel(x_ref, g_ref, w_ref, lb_ref, q_ref, lf_ref, k_ref, i_ref, sg_ref, *, d_model):
    h = _rmsnorm(x_ref[0], g_ref[...]).astype(BF16)
    D = d_model
    q = jnp.dot(h, w_ref[:, 0:D], preferred_element_type=F32)
    q_ref[0] = _silu(q).astype(BF16)
    fl = jnp.dot(h, w_ref[:, D:2 * D], preferred_element_type=F32)
    lb = lb_ref[...]
    f = lb + (1.0 - lb) * (1.0 / (1.0 + jnp.exp(-fl)))
    lf_ref[0] = jnp.log(f)
    k_ref[0] = (1.0 - f).astype(BF16)
    i_ref[0] = jnp.dot(h, w_ref[:, 2 * D:3 * D], preferred_element_type=F32).astype(BF16)
    gg = jnp.dot(h, w_ref[:, 3 * D:4 * D], preferred_element_type=F32)
    sg_ref[0] = _silu(gg).astype(BF16)


def _hgrn_in(x, g, w, lb, *, ts):
    B, S, D = x.shape
    const = lambda b, s: (0, 0)
    tile = pl.BlockSpec((1, ts, D), lambda b, s: (b, s, 0))
    return pl.pallas_call(
        functools.partial(_hgrn_in_kernel, d_model=D),
        grid=(B, S // ts),
        in_specs=[tile, pl.BlockSpec((1, D), const), pl.BlockSpec((D, 4 * D), const),
                  pl.BlockSpec((1, D), const)],
        out_specs=[tile] * 5,
        out_shape=[
            jax.ShapeDtypeStruct((B, S, D), BF16),
            jax.ShapeDtypeStruct((B, S, D), F32),
            jax.ShapeDtypeStruct((B, S, D), BF16),
            jax.ShapeDtypeStruct((B, S, D), BF16),
            jax.ShapeDtypeStruct((B, S, D), BF16),
        ],
        compiler_params=pltpu.CompilerParams(
            dimension_semantics=("arbitrary", "arbitrary"), vmem_limit_bytes=VMEM_LIMIT),
        name="hgrn_in",
    )(x, g, w, lb)


def _chunk_cumsum(x):
    r = lax.broadcasted_iota(jnp.int32, x.shape, 0) & (HGRN_CHUNK - 1)
    k = 1
    while k < HGRN_CHUNK:
        x = x + jnp.where(r >= k, pltpu.roll(x, k, axis=0), 0.0)
        k *= 2
    return x


def _hgrn_core_kernel(q_ref, *refs, ts):
    def tile(s, st):
        return _hgrn_core_tile(s, st, q_ref, *refs, ts=ts)

    lax.fori_loop(0, q_ref.shape[1] // ts, tile,
                  jnp.zeros((HGRN_HEAD_DIM, HGRN_HEAD_DIM), F32))


def _hgrn_core_tile(s, st, q_ref, lf_ref, k_ref, i_ref, sg_ref, og_ref, o_ref, *, ts):
    C = HGRN_CHUNK
    r0 = pl.multiple_of(s * ts, ts)
    rows = pl.ds(r0, ts)
    b_all = _chunk_cumsum(lf_ref[0, rows, :])
    q_all = q_ref[0, rows, :].astype(F32)
    k_all = k_ref[0, rows, :].astype(F32)
    q_dec_all = (q_all * jnp.exp(b_all)).astype(BF16)
    k_inv_all = (k_all * jnp.exp(-b_all)).astype(BF16)
    iv_all = i_ref[0, rows, :]
    crow = lax.broadcasted_iota(jnp.int32, (C, C), 0)
    ccol = lax.broadcasted_iota(jnp.int32, (C, C), 1)

    n_chunks = ts // C
    sls = [slice(n * C, (n + 1) * C) for n in range(n_chunks)]
    b_lasts = [b_all[(n + 1) * C - 1:(n + 1) * C, :] for n in range(n_chunks)]
    a_list, ut_list = [], []
    for n, sl in enumerate(sls):
        a_list.append(lax.dot_general(q_dec_all[sl, :], k_inv_all[sl, :], (((1,), (1,)), ((), ())),
                                      preferred_element_type=F32))
        k_state = (k_all[sl, :] * jnp.exp(b_lasts[n] - b_all[sl, :])).astype(BF16)
        ut_list.append(lax.dot_general(iv_all[sl, :], k_state, (((0,), (0,)), ((), ())),
                                       preferred_element_type=F32))
    st_list = []
    for n in range(n_chunks):
        st_list.append(st.astype(BF16))
        st = jnp.exp(b_lasts[n]) * st + ut_list[n]
    outs = []
    for n, sl in enumerate(sls):
        a = jnp.where(ccol <= crow, a_list[n], 0.0).astype(BF16)
        o = jnp.dot(a, iv_all[sl, :], preferred_element_type=F32)
        outs.append(o + lax.dot_general(q_dec_all[sl, :], st_list[n], (((1,), (1,)), ((), ())),
                                        preferred_element_type=F32))
    o = jnp.concatenate(outs, axis=0)
    o = o * lax.rsqrt(jnp.mean(o * o, axis=-1, keepdims=True) + RMS_EPS)
    o = o * og_ref[...] * sg_ref[0, rows, :].astype(F32)
    o_ref[0, rows, :] = o.astype(BF16)
    return st


def _hgrn_core(q, lf, k, iv, sg, og, *, ts):
    B, S, D = q.shape
    H = D // HGRN_HEAD_DIM
    seq = pl.BlockSpec((1, S, HGRN_HEAD_DIM), lambda b, h: (b, 0, h))
    return pl.pallas_call(
        functools.partial(_hgrn_core_kernel, ts=ts),
        grid=(B, H),
        in_specs=[seq] * 5 + [pl.BlockSpec((1, HGRN_HEAD_DIM), lambda b, h: (0, h))],
        out_specs=seq,
        out_shape=jax.ShapeDtypeStruct((B, S, D), BF16),
        compiler_params=pltpu.CompilerParams(
            dimension_semantics=("arbitrary", "arbitrary"), vmem_limit_bytes=VMEM_LIMIT),
        name="hgrn_core",
    )(q, lf, k, iv, sg, og)


def kernel(x, att_norm_g, att_w_in, att_b_f, att_w_out, hgrn_norm_g, hgrn_w_in, hgrn_lb_logits,
           hgrn_onorm_g, hgrn_w_out, ffn_norm_g, ffn_w_up, ffn_conv_w, ffn_conv_b, ffn_w_down,
           final_norm_g):
    B, S, D = x.shape
    depth = ffn_norm_g.shape[0]
    assert D == FOX_HEADS * FOX_HEAD_DIM and FOX_HEADS * AUG_LANES == LANES
    assert S % ROW_TILE == 0 and S % ATTN_TILE == 0
    assert S % HGRN_TILE == 0 and HGRN_TILE % HGRN_CHUNK == 0
    assert (ffn_w_up.shape[2] // 2) % FFN_CHUNK == 0

    sm = jax.nn.softmax(hgrn_lb_logits.astype(F32), axis=0)
    lower_bounds = jnp.cumsum(sm, axis=0) - sm[0:1]
    fg = final_norm_g.reshape(1, D)
    hsum = (jnp.arange(D)[:, None] // FOX_HEAD_DIM
            == jnp.arange(LANES)[None, :] // AUG_LANES).astype(BF16)

    for layer in range(depth):
        j = layer // 2
        if layer % 2 == 0:
            order = jnp.argsort(att_b_f[j])
            cols = (order[:, None] * FOX_HEAD_DIM + jnp.arange(FOX_HEAD_DIM)[None, :]).reshape(-1)
            w_in = att_w_in[j]
            wqkv = jnp.concatenate([jnp.take(w_in[:, s * D:(s + 1) * D], cols, axis=1)
                                    for s in range(3)], axis=1).astype(BF16)
            wf = jnp.repeat(jnp.take(w_in[:, 3 * D:], order, axis=1), AUG_LANES,
                            axis=1).astype(BF16)
            bf = jnp.repeat(jnp.take(att_b_f[j], order), AUG_LANES).reshape(1, LANES)
            qkv, augq, augk, stats = _fox_in(x, att_norm_g[j].reshape(1, D), wqkv, wf, bf, hsum,
                                             ts=ATTN_TILE)
            o = _fox_attn(_attn_block_counts(stats), qkv, augq, augk, t=ATTN_TILE)
            w_out = jnp.take(att_w_out[j], cols, axis=0)
        else:
            q, lf, k, iv, sg = _hgrn_in(x, hgrn_norm_g[j].reshape(1, D), hgrn_w_in[j].astype(BF16),
                                        lower_bounds[layer].reshape(1, D), ts=ROW_TILE)
            o = _hgrn_core(q, lf, k, iv, sg, hgrn_onorm_g[j].reshape(1, D), ts=HGRN_TILE)
            w_out = hgrn_w_out[j]
        wg, wv, cg, cv, wd = _ffn_weights(ffn_w_up[layer], ffn_conv_w[layer], ffn_conv_b[layer],
                                          ffn_w_down[layer])
        x = _ffn(x, o, w_out.astype(BF16), ffn_norm_g[layer].reshape(1, D), wg, wv, cg, cv, wd, fg,
                 ts=ROW_TILE, final=(layer == depth - 1))
    return x
```

```python
import functools
import math

import jax
import jax.numpy as jnp
from jax import lax
from jax.experimental import pallas as pl
from jax.experimental.pallas import tpu as pltpu

F32 = jnp.float32
BF16 = jnp.bfloat16

RMS_EPS = 1e-6
FOX_HEADS = 16
FOX_HEAD_DIM = 64
HGRN_HEAD_DIM = 128
HGRN_CHUNK = 64
CONV_WIDTH = 3
LOG2E = math.log2(math.e)

LANES = 128
AUG_LANES = 8
NEG_BIG = -1e30
NORM_SLACK = 1.01
EXP2_ZERO_GAP = 160.0
VMEM_LIMIT = 56 * 1024 * 1024

ROW_TILE = 512
ATTN_TILE = 512
HGRN_TILE = 1024
FFN_CHUNK = 256
FFN_STAGES = 3


def _rmsnorm(x, g):
    ms = jnp.mean(x * x, axis=-1, keepdims=True)
    return x * lax.rsqrt(ms + RMS_EPS) * g


def _split3(x):
    hi = x.astype(BF16)
    r = x - hi.astype(F32)
    mid = r.astype(BF16)
    lo = (r - mid.astype(F32)).astype(BF16)
    return hi, mid, lo


def _silu(x):
    return x * (1.0 / (1.0 + jnp.exp(-x)))


def _tile_lanes(x, n):
    return jnp.concatenate([x] * n, axis=1) if n > 1 else x


def _fox_in_kernel(x_ref, g_ref, wqkv_ref, wf_ref, bf_ref, hsum_ref, qkv_ref, augq_ref, augk_ref,
                   stats_ref, carry_ref, *, d_model, q_scale):
    ts = x_ref.shape[1]

    @pl.when(pl.program_id(1) == 0)
    def _():
        carry_ref[...] = jnp.zeros_like(carry_ref)

    h = _rmsnorm(x_ref[0], g_ref[...]).astype(BF16)
    norm2_max = []
    for c in range(3):
        acc = jnp.dot(h, wqkv_ref[:, c * d_model:(c + 1) * d_model], preferred_element_type=F32)
        if c == 0:
            acc = acc * q_scale
        yb = acc.astype(BF16)
        qkv_ref[0, :, c * d_model:(c + 1) * d_model] = yb
        if c < 2:
            sq = jnp.square(yb.astype(F32)).astype(BF16)
            n2 = jnp.dot(sq, hsum_ref[...], preferred_element_type=F32) * NORM_SLACK
            norm2_max.append(jnp.max(n2, axis=0, keepdims=True))

    z = jnp.dot(h, wf_ref[...], preferred_element_type=F32) + bf_ref[...]
    logf = jnp.minimum(z, 0.0) - jnp.log1p(jnp.exp(-jnp.abs(z)))
    lc = logf * LOG2E
    row = lax.broadcasted_iota(jnp.int32, (ts, ts), 0)
    col = lax.broadcasted_iota(jnp.int32, (ts, ts), 1)
    tri = jnp.where(col <= row, 1.0, 0.0).astype(BF16)
    hi, mid, lo = _split3(lc)
    cs = (jnp.dot(tri, hi, preferred_element_type=F32)
          + jnp.dot(tri, mid, preferred_element_type=F32)
          + jnp.dot(tri, lo, preferred_element_type=F32)) + carry_ref[...]
    carry_ref[...] = cs[ts - 1:ts, :]

    c_hi, c_mid, c_lo = (t.astype(F32) for t in _split3(cs))
    j = lax.broadcasted_iota(jnp.int32, (ts, LANES), 1) & (AUG_LANES - 1)
    augq = jnp.where(j == 0, c_hi, jnp.where(j == 1, c_mid, jnp.where(j == 2, c_lo,
                     jnp.where(j < 6, 1.0, 0.0))))
    augk = jnp.where(j < 3, 1.0, jnp.where(j == 3, -c_hi, jnp.where(j == 4, -c_mid,
                     jnp.where(j == 5, -c_lo, 0.0))))
    augq_ref[0] = augq.astype(BF16)
    augk_ref[0] = augk.astype(BF16)

    stats_ref[0, 0] = jnp.concatenate(
        [norm2_max[0], norm2_max[1], cs[0:1, :], cs[ts - 1:ts, :], jnp.zeros((4, LANES), F32)],
        axis=0)


def _fox_in(x, g, wqkv, wf, bf, hsum, *, ts):
    B, S, D = x.shape
    q_scale = LOG2E / math.sqrt(FOX_HEAD_DIM)
    const = lambda b, s: (0, 0)
    return pl.pallas_call(
        functools.partial(_fox_in_kernel, d_model=D, q_scale=q_scale),
        grid=(B, S // ts),
        in_specs=[
            pl.BlockSpec((1, ts, D), lambda b, s: (b, s, 0)),
            pl.BlockSpec((1, D), const),
            pl.BlockSpec((D, 3 * D), const),
            pl.BlockSpec((D, LANES), const),
            pl.BlockSpec((1, LANES), const),
            pl.BlockSpec((D, LANES), const),
        ],
        out_specs=[
            pl.BlockSpec((1, ts, 3 * D), lambda b, s: (b, s, 0)),
            pl.BlockSpec((1, ts, LANES), lambda b, s: (b, s, 0)),
            pl.BlockSpec((1, ts, LANES), lambda b, s: (b, s, 0)),
            pl.BlockSpec((1, 1, 8, LANES), lambda b, s: (b, s, 0, 0)),
        ],
        out_shape=[
            jax.ShapeDtypeStruct((B, S, 3 * D), BF16),
            jax.ShapeDtypeStruct((B, S, LANES), BF16),
            jax.ShapeDtypeStruct((B, S, LANES), BF16),
            jax.ShapeDtypeStruct((B, S // ts, 8, LANES), F32),
        ],
        scratch_shapes=[pltpu.VMEM((1, LANES), F32)],
        compiler_params=pltpu.CompilerParams(
            dimension_semantics=("arbitrary", "arbitrary"), vmem_limit_bytes=VMEM_LIMIT),
        name="fox_in",
    )(x, g, wqkv, wf, bf, hsum)


def _attn_block_counts(stats):
    B, nt = stats.shape[0], stats.shape[1]
    per_head = stats[:, :, :4, ::AUG_LANES]
    qn2, kn2, c_first, c_last = (per_head[:, :, r] for r in range(4))
    kn2_max = jnp.max(kn2, axis=1, keepdims=True)
    bound = 2.0 * jnp.sqrt(qn2 * kn2_max) + c_first + 1.0
    gap = bound[:, :, None, :] - c_last[:, None, :, :]
    kk = lax.broadcasted_iota(jnp.int32, (nt, nt), 1)
    ii = lax.broadcasted_iota(jnp.int32, (nt, nt), 0)
    need = (gap > -EXP2_ZERO_GAP) & (kk < ii)[None, :, :, None]
    oldest = jnp.min(jnp.where(need, kk[None, :, :, None], nt), axis=2)
    count = jnp.maximum(jnp.arange(nt)[None, :, None] - oldest, 0) + 1
    pairs = jnp.max(count.reshape(B, nt, -1, 2), axis=-1)
    return jnp.transpose(pairs, (0, 2, 1)).reshape(-1).astype(jnp.int32)


def _fox_attn_kernel(nblk_ref, q_ref, *refs, t):
    j = pl.program_id(1)
    nq = q_ref.shape[1] // t

    def tile(i, carry):
        n = nblk_ref[(pl.program_id(0) * pl.num_programs(1) + j) * nq + i]
        _fox_attn_tile(i, n, j, q_ref, *refs, t=t)
        return carry

    lax.fori_loop(0, nq, tile, 0)


def _fox_attn_tile(i, n, j, q_ref, augq_ref, k_ref, v_ref, augk_ref, o_ref, lhs_ref, m_ref,
                   acc_ref, sa_ref, sb_ref, pa_ref, pb_ref, ala_ref, alb_ref, mxa_ref, mxb_ref,
                   *, t):
    lane = lax.broadcasted_iota(jnp.int32, (t, LANES), 1)
    q_rows = pl.ds(pl.multiple_of(i * t, t), t)
    q2 = q_ref[0, q_rows, :]
    aq = augq_ref[0, q_rows, :]
    zero = jnp.zeros((), BF16)
    for hh in range(2):
        lo = (2 * j + hh) * AUG_LANES
        qm = jnp.where((lane >= FOX_HEAD_DIM * hh) & (lane < FOX_HEAD_DIM * (hh + 1)), q2, zero)
        am = jnp.where((lane >= lo) & (lane < lo + AUG_LANES), aq, zero)
        lhs_ref[hh] = jnp.concatenate([qm, am], axis=1)
        m_ref[hh] = jnp.full((t, LANES), NEG_BIG, F32)
        acc_ref[hh] = jnp.zeros((t, LANES), F32)

    one = jnp.ones((), BF16)
    row = lax.broadcasted_iota(jnp.int32, (t, t), 0)
    col = lax.broadcasted_iota(jnp.int32, (t, t), 1)

    def scores(jb, buf, masked=False):
        s_ref, mx_ref, _, _ = buf
        off = pl.multiple_of((i - jb) * t, t)
        rhs = jnp.concatenate([k_ref[0, pl.ds(off, t), :], augk_ref[0, pl.ds(off, t), :]], axis=1)
        for hh in range(2):
            s = lax.dot_general(lhs_ref[hh], rhs, (((1,), (1,)), ((), ())),
                                preferred_element_type=F32)
            if masked:
                s = jnp.where(col <= row, s, NEG_BIG)
            s_ref[hh] = s
            mx = s[:, :LANES]
            for c in range(1, t // LANES):
                mx = jnp.maximum(mx, s[:, c * LANES:(c + 1) * LANES])
            mx_ref[hh] = mx

    def softmax(buf):
        s_ref, mx_ref, p_ref, al_ref = buf
        for hh in range(2):
            m_prev = m_ref[hh]
            m_new = jnp.maximum(m_prev, jnp.max(mx_ref[hh], axis=1, keepdims=True))
            al_ref[hh] = jnp.exp2(m_prev - m_new)
            p_ref[hh] = jnp.exp2(s_ref[hh] - _tile_lanes(m_new, t // LANES)).astype(BF16)
            m_ref[hh] = m_new

    def values(jb, buf):
        _, _, p_ref, al_ref = buf
        off = pl.multiple_of((i - jb) * t, t)
        v2 = v_ref[0, pl.ds(off, t), :]
        for hh in range(2):
            own = (lane >= FOX_HEAD_DIM * hh) & (lane < FOX_HEAD_DIM * (hh + 1))
            pv = jnp.dot(p_ref[hh], jnp.where(own, v2, one), preferred_element_type=F32)
            acc_ref[hh] = al_ref[hh] * acc_ref[hh] + pv

    n_steady = jnp.maximum(n - 2, 0)

    buf_a = (sa_ref, mxa_ref, pa_ref, ala_ref)
    buf_b = (sb_ref, mxb_ref, pb_ref, alb_ref)

    @pl.when(n == 1)
    def _():
        scores(0, buf_a, masked=True)
        softmax(buf_a)
        values(0, buf_a)

    @pl.when(n > 1)
    def _():
        scores(0, buf_a, masked=True)
        softmax(buf_a)
        scores(1, buf_b)

    def step_pair(u, carry):
        tau = 2 + 2 * u
        values(tau - 2, buf_a)
        softmax(buf_b)
        scores(tau, buf_a)
        values(tau - 1, buf_b)
        softmax(buf_a)
        scores(tau + 1, buf_b)
        return carry

    pairs = n_steady // 2
    lax.fori_loop(0, jnp.maximum(pairs - 1, 0), step_pair, 0)

    def drain_even():
        values(n - 2, buf_a)
        softmax(buf_b)
        values(n - 1, buf_b)

    def drain_odd():
        values(n - 3, buf_a)
        softmax(buf_b)
        scores(n - 1, buf_a)
        values(n - 2, buf_b)
        softmax(buf_a)
        values(n - 1, buf_a)

    for parity, drain in ((0, drain_even), (1, drain_odd)):
        @pl.when((n % 2 == parity) & (n >= 2) & (pairs >= 1))
        def _():
            step_pair(pairs - 1, 0)
            drain()

        @pl.when((n % 2 == parity) & (n >= 2) & (pairs == 0))
        def _():
            drain()

    out_a = acc_ref[0] / pltpu.roll(acc_ref[0], FOX_HEAD_DIM, axis=1)
    out_b = acc_ref[1] / pltpu.roll(acc_ref[1], FOX_HEAD_DIM, axis=1)
    o_ref[0, q_rows, :] = jnp.where(lane < FOX_HEAD_DIM, out_a, out_b).astype(BF16)


def _fox_attn(nblk, qkv, augq, augk, *, t):
    B, S, D3 = qkv.shape
    D = D3 // 3
    nb = D // LANES
    grid_spec = pltpu.PrefetchScalarGridSpec(
        num_scalar_prefetch=1,
        grid=(B, nb),
        in_specs=[
            pl.BlockSpec((1, S, LANES), lambda b, j, nblk: (b, 0, j)),
            pl.BlockSpec((1, S, LANES), lambda b, j, nblk: (b, 0, 0)),
            pl.BlockSpec((1, S, LANES), lambda b, j, nblk: (b, 0, nb + j)),
            pl.BlockSpec((1, S, LANES), lambda b, j, nblk: (b, 0, 2 * nb + j)),
            pl.BlockSpec((1, S, LANES), lambda b, j, nblk: (b, 0, 0)),
        ],
        out_specs=pl.BlockSpec((1, S, LANES), lambda b, j, nblk: (b, 0, j)),
        scratch_shapes=[
            pltpu.VMEM((2, t, 2 * LANES), BF16),
            pltpu.VMEM((2, t, LANES), F32),
            pltpu.VMEM((2, t, LANES), F32),
            pltpu.VMEM((2, t, t), F32),
            pltpu.VMEM((2, t, t), F32),
            pltpu.VMEM((2, t, t), BF16),
            pltpu.VMEM((2, t, t), BF16),
            pltpu.VMEM((2, t, LANES), F32),
            pltpu.VMEM((2, t, LANES), F32),
            pltpu.VMEM((2, t, LANES), F32),
            pltpu.VMEM((2, t, LANES), F32),
        ],
    )
    return pl.pallas_call(
        functools.partial(_fox_attn_kernel, t=t),
        grid_spec=grid_spec,
        out_shape=jax.ShapeDtypeStruct((B, S, D), BF16),
        compiler_params=pltpu.CompilerParams(
            dimension_semantics=("arbitrary", "arbitrary"), vmem_limit_bytes=VMEM_LIMIT),
        name="fox_attn",
    )(nblk, qkv, augq, qkv, qkv, augk)


def _ffn_kernel(x_ref, o_ref, wout_ref, g_ref, wg_ref, wv_ref, cg_ref, cv_ref, wd_ref, fg_ref,
                out_ref, tail_ref, x1_ref, h_ref, acc_ref, u_ref, z_ref, *, final):
    ts = x_ref.shape[1]
    n_chunks = wd_ref.shape[0] // FFN_CHUNK

    def cols(c):
        return slice(c * FFN_CHUNK, (c + 1) * FFN_CHUNK)

    @pl.when(pl.program_id(1) == 0)
    def _():
        tail_ref[...] = jnp.zeros_like(tail_ref)

    x1 = x_ref[0] + jnp.dot(o_ref[0], wout_ref[...], preferred_element_type=F32)
    x1_ref[...] = x1
    h_ref[...] = _rmsnorm(x1, g_ref[...]).astype(BF16)
    acc_ref[...] = jnp.zeros_like(acc_ref)
    def up(c):
        h = h_ref[...]
        for stream, w_ref in enumerate((wg_ref, wv_ref)):
            u = jnp.dot(h, w_ref[:, cols(c)], preferred_element_type=F32)
            u_ref[c % FFN_STAGES, stream, 0:8, :] = tail_ref[stream, c]
            u_ref[c % FFN_STAGES, stream, 8:, :] = u
            tail_ref[stream, c] = u[ts - 8:, :]

    def conv(c, stream, cw):
        u0 = u_ref[c % FFN_STAGES, stream, 8:8 + ts, :]
        u1 = u_ref[c % FFN_STAGES, stream, 7:7 + ts, :]
        u2 = u_ref[c % FFN_STAGES, stream, 6:6 + ts, :]
        return cw[3:4, :] + cw[0:1, :] * u2 + cw[1:2, :] * u1 + cw[2:3, :] * u0

    def act(c):
        gate = conv(c, 0, cg_ref[:, cols(c)])
        val = conv(c, 1, cv_ref[:, cols(c)])
        z_ref[c % FFN_STAGES] = (_silu(gate) * val).astype(BF16)

    def down(c):
        acc_ref[...] += jnp.dot(z_ref[c % FFN_STAGES], wd_ref[cols(c), :],
                                preferred_element_type=F32)

    up(0)
    if n_chunks > 1:
        up(1)
    act(0)
    for c in range(n_chunks):
        if c + 2 < n_chunks:
            up(c + 2)
        if c + 1 < n_chunks:
            act(c + 1)
        down(c)
    y = x1_ref[...] + acc_ref[...]
    if final:
        y = _rmsnorm(y, fg_ref[...])
    out_ref[0] = y


def _ffn(x, o, wout, g, wg, wv, cg, cv, wd, fg, *, ts, final):
    B, S, D = x.shape
    n_chunks = wd.shape[0] // FFN_CHUNK
    c2 = lambda b, s: (0, 0)
    return pl.pallas_call(
        functools.partial(_ffn_kernel, final=final),
        grid=(B, S // ts),
        in_specs=[
            pl.BlockSpec((1, ts, D), lambda b, s: (b, s, 0)),
            pl.BlockSpec((1, ts, D), lambda b, s: (b, s, 0)),
            pl.BlockSpec((D, D), c2),
            pl.BlockSpec((1, D), c2),
            pl.BlockSpec(wg.shape, c2),
            pl.BlockSpec(wv.shape, c2),
            pl.BlockSpec(cg.shape, c2),
            pl.BlockSpec(cv.shape, c2),
            pl.BlockSpec(wd.shape, c2),
            pl.BlockSpec((1, D), c2),
        ],
        out_specs=pl.BlockSpec((1, ts, D), lambda b, s: (b, s, 0)),
        out_shape=jax.ShapeDtypeStruct((B, S, D), F32),
        scratch_shapes=[
            pltpu.VMEM((2, n_chunks, 8, FFN_CHUNK), F32),
            pltpu.VMEM((ts, D), F32),
            pltpu.VMEM((ts, D), BF16),
            pltpu.VMEM((ts, D), F32),
            pltpu.VMEM((FFN_STAGES, 2, 8 + ts, FFN_CHUNK), F32),
            pltpu.VMEM((FFN_STAGES, ts, FFN_CHUNK), BF16),
        ],
        compiler_params=pltpu.CompilerParams(
            dimension_semantics=("arbitrary", "arbitrary"), vmem_limit_bytes=VMEM_LIMIT),
        name="ffn_final" if final else "ffn",
    )(x, o, wout, g, wg, wv, cg, cv, wd, fg)


def _ffn_weights(w_up, conv_w, conv_b, w_down):
    F = w_up.shape[1] // 2

    def taps(w, b):
        return jnp.concatenate([w, b[None, :], jnp.zeros((8 - CONV_WIDTH - 1, F), F32)], axis=0)

    return (w_up[:, :F].astype(BF16), w_up[:, F:].astype(BF16),
            taps(conv_w[:, :F], conv_b[:F]), taps(conv_w[:, F:], conv_b[F:]),
            w_down.astype(BF16))


def _hgrn_in_kernel(x_ref, g_ref, w_ref, lb_ref, q_ref, lf_ref, k_ref, i_ref, sg_ref, *, d_model):
    h = _rmsnorm(x_ref[0], g_ref[...]).astype(BF16)
    D = d_model
    q = jnp.dot(h, w_ref[:, 0:D], preferred_element_type=F32)
    q_ref[0] = _silu(q).astype(BF16)
    fl = jnp.dot(h, w_ref[:, D:2 * D], preferred_element_type=F32)
    lb = lb_ref[...]
    f = lb + (1.0 - lb) * (1.0 / (1.0 + jnp.exp(-fl)))
    lf_ref[0] = jnp.log(f)
    k_ref[0] = (1.0 - f).astype(BF16)
    i_ref[0] = jnp.dot(h, w_ref[:, 2 * D:3 * D], preferred_element_type=F32).astype(BF16)
    gg = jnp.dot(h, w_ref[:, 3 * D:4 * D], preferred_element_type=F32)
    sg_ref[0] = _silu(gg).astype(BF16)


def _hgrn_in(x, g, w, lb, *, ts):
    B, S, D = x.shape
    const = lambda b, s: (0, 0)
    tile = pl.BlockSpec((1, ts, D), lambda b, s: (b, s, 0))
    return pl.pallas_call(
        functools.partial(_hgrn_in_kernel, d_model=D),
        grid=(B, S // ts),
        in_specs=[tile, pl.BlockSpec((1, D), const), pl.BlockSpec((D, 4 * D), const),
                  pl.BlockSpec((1, D), const)],
        out_specs=[tile] * 5,
        out_shape=[
            jax.ShapeDtypeStruct((B, S, D), BF16),
            jax.ShapeDtypeStruct((B, S, D), F32),
            jax.ShapeDtypeStruct((B, S, D), BF16),
            jax.ShapeDtypeStruct((B, S, D), BF16),
            jax.ShapeDtypeStruct((B, S, D), BF16),
        ],
        compiler_params=pltpu.CompilerParams(
            dimension_semantics=("arbitrary", "arbitrary"), vmem_limit_bytes=VMEM_LIMIT),
        name="hgrn_in",
    )(x, g, w, lb)


def _chunk_cumsum(x):
    r = lax.broadcasted_iota(jnp.int32, x.shape, 0) & (HGRN_CHUNK - 1)
    k = 1
    while k < HGRN_CHUNK:
        x = x + jnp.where(r >= k, pltpu.roll(x, k, axis=0), 0.0)
        k *= 2
    return x


def _hgrn_core_kernel(q_ref, *refs, ts):
    def tile(s, st):
        return _hgrn_core_tile(s, st, q_ref, *refs, ts=ts)

    lax.fori_loop(0, q_ref.shape[1] // ts, tile,
                  jnp.zeros((HGRN_HEAD_DIM, HGRN_HEAD_DIM), F32))


def _hgrn_core_tile(s, st, q_ref, lf_ref, k_ref, i_ref, sg_ref, og_ref, o_ref, *, ts):
    C = HGRN_CHUNK
    r0 = pl.multiple_of(s * ts, ts)
    rows = pl.ds(r0, ts)
    b_all = _chunk_cumsum(lf_ref[0, rows, :])
    q_all = q_ref[0, rows, :].astype(F32)
    k_all = k_ref[0, rows, :].astype(F32)
    q_dec_all = (q_all * jnp.exp(b_all)).astype(BF16)
    k_inv_all = (k_all * jnp.exp(-b_all)).astype(BF16)
    iv_all = i_ref[0, rows, :]
    crow = lax.broadcasted_iota(jnp.int32, (C, C), 0)
    ccol = lax.broadcasted_iota(jnp.int32, (C, C), 1)

    n_chunks = ts // C
    sls = [slice(n * C, (n + 1) * C) for n in range(n_chunks)]
    b_lasts = [b_all[(n + 1) * C - 1:(n + 1) * C, :] for n in range(n_chunks)]
    a_list, ut_list = [], []
    for n, sl in enumerate(sls):
        a_list.append(lax.dot_general(q_dec_all[sl, :], k_inv_all[sl, :], (((1,), (1,)), ((), ())),
                                      preferred_element_type=F32))
        k_state = (k_all[sl, :] * jnp.exp(b_lasts[n] - b_all[sl, :])).astype(BF16)
        ut_list.append(lax.dot_general(iv_all[sl, :], k_state, (((0,), (0,)), ((), ())),
                                       preferred_element_type=F32))
    st_list = []
    for n in range(n_chunks):
        st_list.append(st.astype(BF16))
        st = jnp.exp(b_lasts[n]) * st + ut_list[n]
    outs = []
    for n, sl in enumerate(sls):
        a = jnp.where(ccol <= crow, a_list[n], 0.0).astype(BF16)
        o = jnp.dot(a, iv_all[sl, :], preferred_element_type=F32)
        outs.append(o + lax.dot_general(q_dec_all[sl, :], st_list[n], (((1,), (1,)), ((), ())),
                                        preferred_element_type=F32))
    o = jnp.concatenate(outs, axis=0)
    o = o * lax.rsqrt(jnp.mean(o * o, axis=-1, keepdims=True) + RMS_EPS)
    o = o * og_ref[...] * sg_ref[0, rows, :].astype(F32)
    o_ref[0, rows, :] = o.astype(BF16)
    return st


def _hgrn_core(q, lf, k, iv, sg, og, *, ts):
    B, S, D = q.shape
    H = D // HGRN_HEAD_DIM
    seq = pl.BlockSpec((1, S, HGRN_HEAD_DIM), lambda b, h: (b, 0, h))
    return pl.pallas_call(
        functools.partial(_hgrn_core_kernel, ts=ts),
        grid=(B, H),
        in_specs=[seq] * 5 + [pl.BlockSpec((1, HGRN_HEAD_DIM), lambda b, h: (0, h))],
        out_specs=seq,
        out_shape=jax.ShapeDtypeStruct((B, S, D), BF16),
        compiler_params=pltpu.CompilerParams(
            dimension_semantics=("arbitrary", "arbitrary"), vmem_limit_bytes=VMEM_LIMIT),
        name="hgrn_core",
    )(q, lf, k, iv, sg, og)


def kernel(x, att_norm_g, att_w_in, att_b_f, att_w_out, hgrn_norm_g, hgrn_w_in, hgrn_lb_logits,
           hgrn_onorm_g, hgrn_w_out, ffn_norm_g, ffn_w_up, ffn_conv_w, ffn_conv_b, ffn_w_down,
           final_norm_g):
    B, S, D = x.shape
    depth = ffn_norm_g.shape[0]
    assert D == FOX_HEADS * FOX_HEAD_DIM and FOX_HEADS * AUG_LANES == LANES
    assert S % ROW_TILE == 0 and S % ATTN_TILE == 0
    assert S % HGRN_TILE == 0 and HGRN_TILE % HGRN_CHUNK == 0
    assert (ffn_w_up.shape[2] // 2) % FFN_CHUNK == 0

    sm = jax.nn.softmax(hgrn_lb_logits.astype(F32), axis=0)
    lower_bounds = jnp.cumsum(sm, axis=0) - sm[0:1]
    fg = final_norm_g.reshape(1, D)
    hsum = (jnp.arange(D)[:, None] // FOX_HEAD_DIM
            == jnp.arange(LANES)[None, :] // AUG_LANES).astype(BF16)

    for layer in range(depth):
        j = layer // 2
        if layer % 2 == 0:
            order = jnp.argsort(att_b_f[j])
            cols = (order[:, None] * FOX_HEAD_DIM + jnp.arange(FOX_HEAD_DIM)[None, :]).reshape(-1)
            w_in = att_w_in[j]
            wqkv = jnp.concatenate([jnp.take(w_in[:, s * D:(s + 1) * D], cols, axis=1)
                                    for s in range(3)], axis=1).astype(BF16)
            wf = jnp.repeat(jnp.take(w_in[:, 3 * D:], order, axis=1), AUG_LANES,
                            axis=1).astype(BF16)
            bf = jnp.repeat(jnp.take(att_b_f[j], order), AUG_LANES).reshape(1, LANES)
            qkv, augq, augk, stats = _fox_in(x, att_norm_g[j].reshape(1, D), wqkv, wf, bf, hsum,
                                             ts=ATTN_TILE)
            o = _fox_attn(_attn_block_counts(stats), qkv, augq, augk, t=ATTN_TILE)
            w_out = jnp.take(att_w_out[j], cols, axis=0)
        else:
            q, lf, k, iv, sg = _hgrn_in(x, hgrn_norm_g[j].reshape(1, D), hgrn_w_in[j].astype(BF16),
                                        lower_bounds[layer].reshape(1, D), ts=HGRN_TILE)
            o = _hgrn_core(q, lf, k, iv, sg, hgrn_onorm_g[j].reshape(1, D), ts=HGRN_TILE)
            w_out = hgrn_w_out[j]
        wg, wv, cg, cv, wd = _ffn_weights(ffn_w_up[layer], ffn_conv_w[layer], ffn_conv_b[layer],
                                          ffn_w_down[layer])
        x = _ffn(x, o, w_out.astype(BF16), ffn_norm_g[layer].reshape(1, D), wg, wv, cg, cv, wd, fg,
                 ts=ROW_TILE, final=(layer == depth - 1))
    return x
```

```python
import functools
import math

import jax
import jax.numpy as jnp
from jax import lax
from jax.experimental import pallas as pl
from jax.experimental.pallas import tpu as pltpu

F32 = jnp.float32
BF16 = jnp.bfloat16

RMS_EPS = 1e-6
FOX_HEADS = 16
FOX_HEAD_DIM = 64
HGRN_HEAD_DIM = 128
HGRN_CHUNK = 64
CONV_WIDTH = 3
LOG2E = math.log2(math.e)

LANES = 128
AUG_LANES = 8
NEG_BIG = -1e30
NORM_SLACK = 1.01
EXP2_ZERO_GAP = 160.0
VMEM_LIMIT = 56 * 1024 * 1024

ROW_TILE = 512
ATTN_TILE = 512
HGRN_TILE = 1024
FFN_CHUNK = 256
FFN_STAGES = 3


def _rmsnorm(x, g):
    ms = jnp.mean(x * x, axis=-1, keepdims=True)
    return x * lax.rsqrt(ms + RMS_EPS) * g


def _split3(x):
    hi = x.astype(BF16)
    r = x - hi.astype(F32)
    mid = r.astype(BF16)
    lo = (r - mid.astype(F32)).astype(BF16)
    return hi, mid, lo


def _silu(x):
    return x * (1.0 / (1.0 + jnp.exp(-x)))


def _tile_lanes(x, n):
    return jnp.concatenate([x] * n, axis=1) if n > 1 else x


def _fox_in_kernel(x_ref, g_ref, wqkv_ref, wf_ref, bf_ref, hsum_ref, qkv_ref, augq_ref, augk_ref,
                   stats_ref, carry_ref, *, d_model, q_scale):
    ts = x_ref.shape[1]

    @pl.when(pl.program_id(1) == 0)
    def _():
        carry_ref[...] = jnp.zeros_like(carry_ref)

    h = _rmsnorm(x_ref[0], g_ref[...]).astype(BF16)
    stored, norm2 = [], []
    for c in range(3):
        acc = jnp.dot(h, wqkv_ref[:, c * d_model:(c + 1) * d_model], preferred_element_type=F32)
        if c == 0:
            acc = acc * q_scale
        yb = acc.astype(BF16)
        qkv_ref[0, :, c * d_model:(c + 1) * d_model] = yb
        if c < 2:
            y32 = yb.astype(F32)
            stored.append(y32)
            sq = jnp.square(y32).astype(BF16)
            norm2.append(jnp.dot(sq, hsum_ref[...], preferred_element_type=F32) * NORM_SLACK)
    diag = jnp.dot((stored[0] * stored[1]).astype(BF16), hsum_ref[...], preferred_element_type=F32)
    diag_lb = diag - jnp.sqrt(norm2[0] * norm2[1]) * (2.0 ** -8)

    z = jnp.dot(h, wf_ref[...], preferred_element_type=F32) + bf_ref[...]
    logf = jnp.minimum(z, 0.0) - jnp.log1p(jnp.exp(-jnp.abs(z)))
    lc = logf * LOG2E
    row = lax.broadcasted_iota(jnp.int32, (ts, ts), 0)
    col = lax.broadcasted_iota(jnp.int32, (ts, ts), 1)
    tri = jnp.where(col <= row, 1.0, 0.0).astype(BF16)
    hi, mid, lo = _split3(lc)
    cs = (jnp.dot(tri, hi, preferred_element_type=F32)
          + jnp.dot(tri, mid, preferred_element_type=F32)
          + jnp.dot(tri, lo, preferred_element_type=F32)) + carry_ref[...]
    carry_ref[...] = cs[ts - 1:ts, :]

    c_hi, c_mid, c_lo = (t.astype(F32) for t in _split3(cs))
    j = lax.broadcasted_iota(jnp.int32, (ts, LANES), 1) & (AUG_LANES - 1)
    augq = jnp.where(j == 0, c_hi, jnp.where(j == 1, c_mid, jnp.where(j == 2, c_lo,
                     jnp.where(j < 6, 1.0, 0.0))))
    augk = jnp.where(j < 3, 1.0, jnp.where(j == 3, -c_hi, jnp.where(j == 4, -c_mid,
                     jnp.where(j == 5, -c_lo, 0.0))))
    augq_ref[0] = augq.astype(BF16)
    augk_ref[0] = augk.astype(BF16)

    stats_ref[0, 0] = jnp.concatenate(
        [jnp.max(norm2[0], axis=0, keepdims=True), jnp.max(norm2[1], axis=0, keepdims=True),
         jnp.max(cs - diag_lb, axis=0, keepdims=True), cs[ts - 1:ts, :],
         jnp.zeros((4, LANES), F32)], axis=0)


def _fox_in(x, g, wqkv, wf, bf, hsum, *, ts):
    B, S, D = x.shape
    q_scale = LOG2E / math.sqrt(FOX_HEAD_DIM)
    const = lambda b, s: (0, 0)
    return pl.pallas_call(
        functools.partial(_fox_in_kernel, d_model=D, q_scale=q_scale),
        grid=(B, S // ts),
        in_specs=[
            pl.BlockSpec((1, ts, D), lambda b, s: (b, s, 0)),
            pl.BlockSpec((1, D), const),
            pl.BlockSpec((D, 3 * D), const),
            pl.BlockSpec((D, LANES), const),
            pl.BlockSpec((1, LANES), const),
            pl.BlockSpec((D, LANES), const),
        ],
        out_specs=[
            pl.BlockSpec((1, ts, 3 * D), lambda b, s: (b, s, 0)),
            pl.BlockSpec((1, ts, LANES), lambda b, s: (b, s, 0)),
            pl.BlockSpec((1, ts, LANES), lambda b, s: (b, s, 0)),
            pl.BlockSpec((1, 1, 8, LANES), lambda b, s: (b, s, 0, 0)),
        ],
        out_shape=[
            jax.ShapeDtypeStruct((B, S, 3 * D), BF16),
            jax.ShapeDtypeStruct((B, S, LANES), BF16),
            jax.ShapeDtypeStruct((B, S, LANES), BF16),
            jax.ShapeDtypeStruct((B, S // ts, 8, LANES), F32),
        ],
        scratch_shapes=[pltpu.VMEM((1, LANES), F32)],
        compiler_params=pltpu.CompilerParams(
            dimension_semantics=("arbitrary", "arbitrary"), vmem_limit_bytes=VMEM_LIMIT),
        name="fox_in",
    )(x, g, wqkv, wf, bf, hsum)


def _attn_block_counts(stats):
    B, nt = stats.shape[0], stats.shape[1]
    per_head = stats[:, :, :4, ::AUG_LANES]
    qn2, kn2, c_minus_diag, c_last = (per_head[:, :, r] for r in range(4))
    kn2_max = jnp.max(kn2, axis=1, keepdims=True)
    bound = jnp.sqrt(qn2 * kn2_max) + c_minus_diag + 1.0
    gap = bound[:, :, None, :] - c_last[:, None, :, :]
    kk = lax.broadcasted_iota(jnp.int32, (nt, nt), 1)
    ii = lax.broadcasted_iota(jnp.int32, (nt, nt), 0)
    need = (gap > -EXP2_ZERO_GAP) & (kk < ii)[None, :, :, None]
    oldest = jnp.min(jnp.where(need, kk[None, :, :, None], nt), axis=2)
    count = jnp.maximum(jnp.arange(nt)[None, :, None] - oldest, 0) + 1
    pairs = jnp.max(count.reshape(B, nt, -1, 2), axis=-1)
    return jnp.transpose(pairs, (0, 2, 1)).reshape(-1).astype(jnp.int32)


def _fox_attn_kernel(nblk_ref, q_ref, *refs, t):
    j = pl.program_id(1)
    nq = q_ref.shape[1] // t

    def tile(i, carry):
        n = nblk_ref[(pl.program_id(0) * pl.num_programs(1) + j) * nq + i]
        _fox_attn_tile(i, n, j, q_ref, *refs, t=t)
        return carry

    lax.fori_loop(0, nq, tile, 0)


def _fox_attn_tile(i, n, j, q_ref, augq_ref, k_ref, v_ref, augk_ref, o_ref, lhs_ref, m_ref,
                   acc_ref, sa_ref, sb_ref, pa_ref, pb_ref, ala_ref, alb_ref, mxa_ref, mxb_ref,
                   *, t):
    lane = lax.broadcasted_iota(jnp.int32, (t, LANES), 1)
    q_rows = pl.ds(pl.multiple_of(i * t, t), t)
    q2 = q_ref[0, q_rows, :]
    aq = augq_ref[0, q_rows, :]
    zero = jnp.zeros((), BF16)
    for hh in range(2):
        lo = (2 * j + hh) * AUG_LANES
        qm = jnp.where((lane >= FOX_HEAD_DIM * hh) & (lane < FOX_HEAD_DIM * (hh + 1)), q2, zero)
        am = jnp.where((lane >= lo) & (lane < lo + AUG_LANES), aq, zero)
        lhs_ref[hh] = jnp.concatenate([qm, am], axis=1)
        m_ref[hh] = jnp.full((t, LANES), NEG_BIG, F32)
        acc_ref[hh] = jnp.zeros((t, LANES), F32)

    one = jnp.ones((), BF16)
    row = lax.broadcasted_iota(jnp.int32, (t, t), 0)
    col = lax.broadcasted_iota(jnp.int32, (t, t), 1)

    def scores(jb, buf, masked=False):
        s_ref, mx_ref, _, _ = buf
        off = pl.multiple_of((i - jb) * t, t)
        rhs = jnp.concatenate([k_ref[0, pl.ds(off, t), :], augk_ref[0, pl.ds(off, t), :]], axis=1)
        for hh in range(2):
            s = lax.dot_general(lhs_ref[hh], rhs, (((1,), (1,)), ((), ())),
                                preferred_element_type=F32)
            if masked:
                s = jnp.where(col <= row, s, NEG_BIG)
            s_ref[hh] = s
            mx = s[:, :LANES]
            for c in range(1, t // LANES):
                mx = jnp.maximum(mx, s[:, c * LANES:(c + 1) * LANES])
            mx_ref[hh] = mx

    def softmax(buf):
        s_ref, mx_ref, p_ref, al_ref = buf
        for hh in range(2):
            m_prev = m_ref[hh]
            m_new = jnp.maximum(m_prev, jnp.max(mx_ref[hh], axis=1, keepdims=True))
            al_ref[hh] = jnp.exp2(m_prev - m_new)
            p_ref[hh] = jnp.exp2(s_ref[hh] - _tile_lanes(m_new, t // LANES)).astype(BF16)
            m_ref[hh] = m_new

    def values(jb, buf):
        _, _, p_ref, al_ref = buf
        off = pl.multiple_of((i - jb) * t, t)
        v2 = v_ref[0, pl.ds(off, t), :]
        for hh in range(2):
            own = (lane >= FOX_HEAD_DIM * hh) & (lane < FOX_HEAD_DIM * (hh + 1))
            pv = jnp.dot(p_ref[hh], jnp.where(own, v2, one), preferred_element_type=F32)
            acc_ref[hh] = al_ref[hh] * acc_ref[hh] + pv

    n_steady = jnp.maximum(n - 2, 0)

    buf_a = (sa_ref, mxa_ref, pa_ref, ala_ref)
    buf_b = (sb_ref, mxb_ref, pb_ref, alb_ref)

    @pl.when(n == 1)
    def _():
        scores(0, buf_a, masked=True)
        softmax(buf_a)
        values(0, buf_a)

    @pl.when(n > 1)
    def _():
        scores(0, buf_a, masked=True)
        softmax(buf_a)
        scores(1, buf_b)

    def step_pair(u, carry):
        tau = 2 + 2 * u
        values(tau - 2, buf_a)
        softmax(buf_b)
        scores(tau, buf_a)
        values(tau - 1, buf_b)
        softmax(buf_a)
        scores(tau + 1, buf_b)
        return carry

    pairs = n_steady // 2
    lax.fori_loop(0, jnp.maximum(pairs - 1, 0), step_pair, 0)

    def drain_even():
        values(n - 2, buf_a)
        softmax(buf_b)
        values(n - 1, buf_b)

    def drain_odd():
        values(n - 3, buf_a)
        softmax(buf_b)
        scores(n - 1, buf_a)
        values(n - 2, buf_b)
        softmax(buf_a)
        values(n - 1, buf_a)

    for parity, drain in ((0, drain_even), (1, drain_odd)):
        @pl.when((n % 2 == parity) & (n >= 2) & (pairs >= 1))
        def _():
            step_pair(pairs - 1, 0)
            drain()

        @pl.when((n % 2 == parity) & (n >= 2) & (pairs == 0))
        def _():
            drain()

    out_a = acc_ref[0] / pltpu.roll(acc_ref[0], FOX_HEAD_DIM, axis=1)
    out_b = acc_ref[1] / pltpu.roll(acc_ref[1], FOX_HEAD_DIM, axis=1)
    o_ref[0, q_rows, :] = jnp.where(lane < FOX_HEAD_DIM, out_a, out_b).astype(BF16)


def _fox_attn(nblk, qkv, augq, augk, *, t):
    B, S, D3 = qkv.shape
    D = D3 // 3
    nb = D // LANES
    grid_spec = pltpu.PrefetchScalarGridSpec(
        num_scalar_prefetch=1,
        grid=(B, nb),
        in_specs=[
            pl.BlockSpec((1, S, LANES), lambda b, j, nblk: (b, 0, j)),
            pl.BlockSpec((1, S, LANES), lambda b, j, nblk: (b, 0, 0)),
            pl.BlockSpec((1, S, LANES), lambda b, j, nblk: (b, 0, nb + j)),
            pl.BlockSpec((1, S, LANES), lambda b, j, nblk: (b, 0, 2 * nb + j)),
            pl.BlockSpec((1, S, LANES), lambda b, j, nblk: (b, 0, 0)),
        ],
        out_specs=pl.BlockSpec((1, S, LANES), lambda b, j, nblk: (b, 0, j)),
        scratch_shapes=[
            pltpu.VMEM((2, t, 2 * LANES), BF16),
            pltpu.VMEM((2, t, LANES), F32),
            pltpu.VMEM((2, t, LANES), F32),
            pltpu.VMEM((2, t, t), F32),
            pltpu.VMEM((2, t, t), F32),
            pltpu.VMEM((2, t, t), BF16),
            pltpu.VMEM((2, t, t), BF16),
            pltpu.VMEM((2, t, LANES), F32),
            pltpu.VMEM((2, t, LANES), F32),
            pltpu.VMEM((2, t, LANES), F32),
            pltpu.VMEM((2, t, LANES), F32),
        ],
    )
    return pl.pallas_call(
        functools.partial(_fox_attn_kernel, t=t),
        grid_spec=grid_spec,
        out_shape=jax.ShapeDtypeStruct((B, S, D), BF16),
        compiler_params=pltpu.CompilerParams(
            dimension_semantics=("arbitrary", "arbitrary"), vmem_limit_bytes=VMEM_LIMIT),
        name="fox_attn",
    )(nblk, qkv, augq, qkv, qkv, augk)


def _ffn_kernel(x_ref, o_ref, wout_ref, g_ref, wup_ref, taps_ref, wd_ref, fg_ref,
                out_ref, tail_ref, x1_ref, h_ref, acc_ref, u_ref, z_ref, *, final):
    ts = x_ref.shape[1]
    ffn_dim = wd_ref.shape[0]
    n_chunks = ffn_dim // FFN_CHUNK

    def cols(c, stream=0):
        lo = stream * ffn_dim + c * FFN_CHUNK
        return slice(lo, lo + FFN_CHUNK)

    @pl.when(pl.program_id(1) == 0)
    def _():
        tail_ref[...] = jnp.zeros_like(tail_ref)

    x1 = x_ref[0] + jnp.dot(o_ref[0], wout_ref[...], preferred_element_type=F32)
    x1_ref[...] = x1
    h_ref[...] = _rmsnorm(x1, g_ref[...]).astype(BF16)
    acc_ref[...] = jnp.zeros_like(acc_ref)
    def up(c):
        h = h_ref[...]
        for stream in range(2):
            u = jnp.dot(h, wup_ref[:, cols(c, stream)], preferred_element_type=F32)
            u_ref[c % FFN_STAGES, stream, 0:8, :] = tail_ref[stream, c]
            u_ref[c % FFN_STAGES, stream, 8:, :] = u
            tail_ref[stream, c] = u[ts - 8:, :]

    def conv(c, stream, cw):
        u0 = u_ref[c % FFN_STAGES, stream, 8:8 + ts, :]
        u1 = u_ref[c % FFN_STAGES, stream, 7:7 + ts, :]
        u2 = u_ref[c % FFN_STAGES, stream, 6:6 + ts, :]
        return cw[3:4, :] + cw[0:1, :] * u2 + cw[1:2, :] * u1 + cw[2:3, :] * u0

    def act(c):
        gate = conv(c, 0, taps_ref[:, cols(c, 0)])
        val = conv(c, 1, taps_ref[:, cols(c, 1)])
        z_ref[c % FFN_STAGES] = (_silu(gate) * val).astype(BF16)

    def down(c):
        acc_ref[...] += jnp.dot(z_ref[c % FFN_STAGES], wd_ref[cols(c), :],
                                preferred_element_type=F32)

    up(0)
    if n_chunks > 1:
        up(1)
    act(0)
    for c in range(n_chunks):
        if c + 2 < n_chunks:
            up(c + 2)
        if c + 1 < n_chunks:
            act(c + 1)
        down(c)
    y = x1_ref[...] + acc_ref[...]
    if final:
        y = _rmsnorm(y, fg_ref[...])
    out_ref[0] = y


def _ffn(x, o, wout, g, wup, taps, wd, fg, *, ts, final):
    B, S, D = x.shape
    n_chunks = wd.shape[0] // FFN_CHUNK
    c2 = lambda b, s: (0, 0)
    return pl.pallas_call(
        functools.partial(_ffn_kernel, final=final),
        grid=(B, S // ts),
        in_specs=[
            pl.BlockSpec((1, ts, D), lambda b, s: (b, s, 0)),
            pl.BlockSpec((1, ts, D), lambda b, s: (b, s, 0)),
            pl.BlockSpec((D, D), c2),
            pl.BlockSpec((1, D), c2),
            pl.BlockSpec(wup.shape, c2),
            pl.BlockSpec(taps.shape, c2),
            pl.BlockSpec(wd.shape, c2),
            pl.BlockSpec((1, D), c2),
        ],
        out_specs=pl.BlockSpec((1, ts, D), lambda b, s: (b, s, 0)),
        out_shape=jax.ShapeDtypeStruct((B, S, D), F32),
        scratch_shapes=[
            pltpu.VMEM((2, n_chunks, 8, FFN_CHUNK), F32),
            pltpu.VMEM((ts, D), F32),
            pltpu.VMEM((ts, D), BF16),
            pltpu.VMEM((ts, D), F32),
            pltpu.VMEM((FFN_STAGES, 2, 8 + ts, FFN_CHUNK), F32),
            pltpu.VMEM((FFN_STAGES, ts, FFN_CHUNK), BF16),
        ],
        compiler_params=pltpu.CompilerParams(
            dimension_semantics=("arbitrary", "arbitrary"), vmem_limit_bytes=VMEM_LIMIT),
        name="ffn_final" if final else "ffn",
    )(x, o, wout, g, wup, taps, wd, fg)


def _ffn_weights(w_up, conv_w, conv_b, w_down):
    taps = jnp.concatenate(
        [conv_w, conv_b[None, :], jnp.zeros((8 - CONV_WIDTH - 1, conv_w.shape[1]), F32)], axis=0)
    return w_up.astype(BF16), taps, w_down.astype(BF16)


def _hgrn_in_kernel(x_ref, g_ref, w_ref, lb_ref, q_ref, lf_ref, k_ref, i_ref, sg_ref, *, d_model):
    h = _rmsnorm(x_ref[0], g_ref[...]).astype(BF16)
    D = d_model
    q = jnp.dot(h, w_ref[:, 0:D], preferred_element_type=F32)
    q_ref[0] = _silu(q).astype(BF16)
    fl = jnp.dot(h, w_ref[:, D:2 * D], preferred_element_type=F32)
    lb = lb_ref[...]
    f = lb + (1.0 - lb) * (1.0 / (1.0 + jnp.exp(-fl)))
    lf_ref[0] = jnp.log(f)
    k_ref[0] = (1.0 - f).astype(BF16)
    i_ref[0] = jnp.dot(h, w_ref[:, 2 * D:3 * D], preferred_element_type=F32).astype(BF16)
    gg = jnp.dot(h, w_ref[:, 3 * D:4 * D], preferred_element_type=F32)
    sg_ref[0] = _silu(gg).astype(BF16)


def _hgrn_in(x, g, w, lb, *, ts):
    B, S, D = x.shape
    const = lambda b, s: (0, 0)
    tile = pl.BlockSpec((1, ts, D), lambda b, s: (b, s, 0))
    return pl.pallas_call(
        functools.partial(_hgrn_in_kernel, d_model=D),
        grid=(B, S // ts),
        in_specs=[tile, pl.BlockSpec((1, D), const), pl.BlockSpec((D, 4 * D), const),
                  pl.BlockSpec((1, D), const)],
        out_specs=[tile] * 5,
        out_shape=[
            jax.ShapeDtypeStruct((B, S, D), BF16),
            jax.ShapeDtypeStruct((B, S, D), F32),
            jax.ShapeDtypeStruct((B, S, D), BF16),
            jax.ShapeDtypeStruct((B, S, D), BF16),
            jax.ShapeDtypeStruct((B, S, D), BF16),
        ],
        compiler_params=pltpu.CompilerParams(
            dimension_semantics=("arbitrary", "arbitrary"), vmem_limit_bytes=VMEM_LIMIT),
        name="hgrn_in",
    )(x, g, w, lb)


def _chunk_cumsum(x):
    r = lax.broadcasted_iota(jnp.int32, x.shape, 0) & (HGRN_CHUNK - 1)
    k = 1
    while k < HGRN_CHUNK:
        x = x + jnp.where(r >= k, pltpu.roll(x, k, axis=0), 0.0)
        k *= 2
    return x


def _hgrn_core_kernel(q_ref, *refs, ts):
    def tile(s, st):
        return _hgrn_core_tile(s, st, q_ref, *refs, ts=ts)

    lax.fori_loop(0, q_ref.shape[1] // ts, tile,
                  jnp.zeros((HGRN_HEAD_DIM, HGRN_HEAD_DIM), F32))


def _hgrn_core_tile(s, st, q_ref, lf_ref, k_ref, i_ref, sg_ref, og_ref, o_ref, *, ts):
    C = HGRN_CHUNK
    r0 = pl.multiple_of(s * ts, ts)
    rows = pl.ds(r0, ts)
    b_all = _chunk_cumsum(lf_ref[0, rows, :])
    q_all = q_ref[0, rows, :].astype(F32)
    k_all = k_ref[0, rows, :].astype(F32)
    q_dec_all = (q_all * jnp.exp(b_all)).astype(BF16)
    k_inv_all = (k_all * jnp.exp(-b_all)).astype(BF16)
    iv_all = i_ref[0, rows, :]
    crow = lax.broadcasted_iota(jnp.int32, (C, C), 0)
    ccol = lax.broadcasted_iota(jnp.int32, (C, C), 1)

    n_chunks = ts // C
    sls = [slice(n * C, (n + 1) * C) for n in range(n_chunks)]
    b_lasts = [b_all[(n + 1) * C - 1:(n + 1) * C, :] for n in range(n_chunks)]
    a_list, ut_list = [], []
    for n, sl in enumerate(sls):
        a_list.append(lax.dot_general(q_dec_all[sl, :], k_inv_all[sl, :], (((1,), (1,)), ((), ())),
                                      preferred_element_type=F32))
        k_state = (k_all[sl, :] * jnp.exp(b_lasts[n] - b_all[sl, :])).astype(BF16)
        ut_list.append(lax.dot_general(iv_all[sl, :], k_state, (((0,), (0,)), ((), ())),
                                       preferred_element_type=F32))
    st_list = []
    for n in range(n_chunks):
        st_list.append(st.astype(BF16))
        st = jnp.exp(b_lasts[n]) * st + ut_list[n]
    outs = []
    for n, sl in enumerate(sls):
        a = jnp.where(ccol <= crow, a_list[n], 0.0).astype(BF16)
        o = jnp.dot(a, iv_all[sl, :], preferred_element_type=F32)
        outs.append(o + lax.dot_general(q_dec_all[sl, :], st_list[n], (((1,), (1,)), ((), ())),
                                        preferred_element_type=F32))
    o = jnp.concatenate(outs, axis=0)
    o = o * lax.rsqrt(jnp.mean(o * o, axis=-1, keepdims=True) + RMS_EPS)
    o = o * og_ref[...] * sg_ref[0, rows, :].astype(F32)
    o_ref[0, rows, :] = o.astype(BF16)
    return st


def _hgrn_core(q, lf, k, iv, sg, og, *, ts):
    B, S, D = q.shape
    H = D // HGRN_HEAD_DIM
    seq = pl.BlockSpec((1, S, HGRN_HEAD_DIM), lambda b, h: (b, 0, h))
    return pl.pallas_call(
        functools.partial(_hgrn_core_kernel, ts=ts),
        grid=(B, H),
        in_specs=[seq] * 5 + [pl.BlockSpec((1, HGRN_HEAD_DIM), lambda b, h: (0, h))],
        out_specs=seq,
        out_shape=jax.ShapeDtypeStruct((B, S, D), BF16),
        compiler_params=pltpu.CompilerParams(
            dimension_semantics=("arbitrary", "arbitrary"), vmem_limit_bytes=VMEM_LIMIT),
        name="hgrn_core",
    )(q, lf, k, iv, sg, og)


def kernel(x, att_norm_g, att_w_in, att_b_f, att_w_out, hgrn_norm_g, hgrn_w_in, hgrn_lb_logits,
           hgrn_onorm_g, hgrn_w_out, ffn_norm_g, ffn_w_up, ffn_conv_w, ffn_conv_b, ffn_w_down,
           final_norm_g):
    B, S, D = x.shape
    depth = ffn_norm_g.shape[0]
    assert D == FOX_HEADS * FOX_HEAD_DIM and FOX_HEADS * AUG_LANES == LANES
    assert S % ROW_TILE == 0 and S % ATTN_TILE == 0
    assert S % HGRN_TILE == 0 and HGRN_TILE % HGRN_CHUNK == 0
    assert (ffn_w_up.shape[2] // 2) % FFN_CHUNK == 0

    sm = jax.nn.softmax(hgrn_lb_logits.astype(F32), axis=0)
    lower_bounds = jnp.cumsum(sm, axis=0) - sm[0:1]
    fg = final_norm_g.reshape(1, D)
    hsum = (jnp.arange(D)[:, None] // FOX_HEAD_DIM
            == jnp.arange(LANES)[None, :] // AUG_LANES).astype(BF16)

    for layer in range(depth):
        j = layer // 2
        if layer % 2 == 0:
            order = jnp.argsort(att_b_f[j])
            cols = (order[:, None] * FOX_HEAD_DIM + jnp.arange(FOX_HEAD_DIM)[None, :]).reshape(-1)
            w_in = att_w_in[j]
            wqkv = jnp.take(w_in[:, :3 * D].reshape(D, 3, FOX_HEADS, FOX_HEAD_DIM), order,
                            axis=2).reshape(D, 3 * D).astype(BF16)
            wf = jnp.repeat(jnp.take(w_in[:, 3 * D:], order, axis=1), AUG_LANES,
                            axis=1).astype(BF16)
            bf = jnp.repeat(jnp.take(att_b_f[j], order), AUG_LANES).reshape(1, LANES)
            qkv, augq, augk, stats = _fox_in(x, att_norm_g[j].reshape(1, D), wqkv, wf, bf, hsum,
                                             ts=ATTN_TILE)
            o = _fox_attn(_attn_block_counts(stats), qkv, augq, augk, t=ATTN_TILE)
            w_out = jnp.take(att_w_out[j], cols, axis=0)
        else:
            q, lf, k, iv, sg = _hgrn_in(x, hgrn_norm_g[j].reshape(1, D), hgrn_w_in[j].astype(BF16),
                                        lower_bounds[layer].reshape(1, D), ts=HGRN_TILE)
            o = _hgrn_core(q, lf, k, iv, sg, hgrn_onorm_g[j].reshape(1, D), ts=HGRN_TILE)
            w_out = hgrn_w_out[j]
        wup, taps, wd = _ffn_weights(ffn_w_up[layer], ffn_conv_w[layer], ffn_conv_b[layer],
                                     ffn_w_down[layer])
        x = _ffn(x, o, w_out.astype(BF16), ffn_norm_g[layer].reshape(1, D), wup, taps, wd, fg,
                 ts=ROW_TILE, final=(layer == depth - 1))
    return x
```

```python
import functools
import math

import jax
import jax.numpy as jnp
from jax import lax
from jax.experimental import pallas as pl
from jax.experimental.pallas import tpu as pltpu

F32 = jnp.float32
BF16 = jnp.bfloat16

RMS_EPS = 1e-6
FOX_HEADS = 16
FOX_HEAD_DIM = 64
HGRN_HEAD_DIM = 128
HGRN_CHUNK = 64
CONV_WIDTH = 3
LOG2E = math.log2(math.e)

LANES = 128
AUG_LANES = 8
NEG_BIG = -1e30
NORM_SLACK = 1.01
EXP2_ZERO_GAP = 160.0
VMEM_LIMIT = 56 * 1024 * 1024

ROW_TILE = 512
ATTN_TILE = 512
HGRN_TILE = 1024
FFN_CHUNK = 256
FFN_UP_LEAD = 3
FFN_STAGES = FFN_UP_LEAD + 1


def _rmsnorm(x, g):
    ms = jnp.mean(x * x, axis=-1, keepdims=True)
    return x * lax.rsqrt(ms + RMS_EPS) * g


def _split3(x):
    hi = x.astype(BF16)
    r = x - hi.astype(F32)
    mid = r.astype(BF16)
    lo = (r - mid.astype(F32)).astype(BF16)
    return hi, mid, lo


def _silu(x):
    return x * (1.0 / (1.0 + jnp.exp(-x)))


def _tile_lanes(x, n):
    return jnp.concatenate([x] * n, axis=1) if n > 1 else x


def _fox_in_kernel(x_ref, g_ref, wqkv_ref, wf_ref, bf_ref, hsum_ref, qkv_ref, augq_ref, augk_ref,
                   stats_ref, carry_ref, *, d_model, q_scale):
    ts = x_ref.shape[1]

    @pl.when(pl.program_id(1) == 0)
    def _():
        carry_ref[...] = jnp.zeros_like(carry_ref)

    h = _rmsnorm(x_ref[0], g_ref[...]).astype(BF16)
    stored, norm2 = [], []
    for c in range(3):
        acc = jnp.dot(h, wqkv_ref[:, c * d_model:(c + 1) * d_model], preferred_element_type=F32)
        if c == 0:
            acc = acc * q_scale
        yb = acc.astype(BF16)
        qkv_ref[0, :, c * d_model:(c + 1) * d_model] = yb
        if c < 2:
            y32 = yb.astype(F32)
            stored.append(y32)
            sq = jnp.square(y32).astype(BF16)
            norm2.append(jnp.dot(sq, hsum_ref[...], preferred_element_type=F32) * NORM_SLACK)
    diag = jnp.dot((stored[0] * stored[1]).astype(BF16), hsum_ref[...], preferred_element_type=F32)
    diag_lb = diag - jnp.sqrt(norm2[0] * norm2[1]) * (2.0 ** -8)

    z = jnp.dot(h, wf_ref[...], preferred_element_type=F32) + bf_ref[...]
    logf = jnp.minimum(z, 0.0) - jnp.log1p(jnp.exp(-jnp.abs(z)))
    lc = logf * LOG2E
    row = lax.broadcasted_iota(jnp.int32, (ts, ts), 0)
    col = lax.broadcasted_iota(jnp.int32, (ts, ts), 1)
    tri = jnp.where(col <= row, 1.0, 0.0).astype(BF16)
    hi, mid, lo = _split3(lc)
    cs = (jnp.dot(tri, hi, preferred_element_type=F32)
          + jnp.dot(tri, mid, preferred_element_type=F32)
          + jnp.dot(tri, lo, preferred_element_type=F32)) + carry_ref[...]
    carry_ref[...] = cs[ts - 1:ts, :]

    c_hi, c_mid, c_lo = (t.astype(F32) for t in _split3(cs))
    j = lax.broadcasted_iota(jnp.int32, (ts, LANES), 1) & (AUG_LANES - 1)
    augq = jnp.where(j == 0, c_hi, jnp.where(j == 1, c_mid, jnp.where(j == 2, c_lo,
                     jnp.where(j < 6, 1.0, 0.0))))
    augk = jnp.where(j < 3, 1.0, jnp.where(j == 3, -c_hi, jnp.where(j == 4, -c_mid,
                     jnp.where(j == 5, -c_lo, 0.0))))
    augq_ref[0] = augq.astype(BF16)
    augk_ref[0] = augk.astype(BF16)

    stats_ref[0, 0] = jnp.concatenate(
        [jnp.max(norm2[0], axis=0, keepdims=True), jnp.max(norm2[1], axis=0, keepdims=True),
         jnp.max(cs - diag_lb, axis=0, keepdims=True), cs[ts - 1:ts, :],
         jnp.zeros((4, LANES), F32)], axis=0)


def _fox_in(x, g, wqkv, wf, bf, hsum, *, ts):
    B, S, D = x.shape
    q_scale = LOG2E / math.sqrt(FOX_HEAD_DIM)
    const = lambda b, s: (0, 0)
    return pl.pallas_call(
        functools.partial(_fox_in_kernel, d_model=D, q_scale=q_scale),
        grid=(B, S // ts),
        in_specs=[
            pl.BlockSpec((1, ts, D), lambda b, s: (b, s, 0)),
            pl.BlockSpec((1, D), const),
            pl.BlockSpec((D, 3 * D), const),
            pl.BlockSpec((D, LANES), const),
            pl.BlockSpec((1, LANES), const),
            pl.BlockSpec((D, LANES), const),
        ],
        out_specs=[
            pl.BlockSpec((1, ts, 3 * D), lambda b, s: (b, s, 0)),
            pl.BlockSpec((1, ts, LANES), lambda b, s: (b, s, 0)),
            pl.BlockSpec((1, ts, LANES), lambda b, s: (b, s, 0)),
            pl.BlockSpec((1, 1, 8, LANES), lambda b, s: (b, s, 0, 0)),
        ],
        out_shape=[
            jax.ShapeDtypeStruct((B, S, 3 * D), BF16),
            jax.ShapeDtypeStruct((B, S, LANES), BF16),
            jax.ShapeDtypeStruct((B, S, LANES), BF16),
            jax.ShapeDtypeStruct((B, S // ts, 8, LANES), F32),
        ],
        scratch_shapes=[pltpu.VMEM((1, LANES), F32)],
        compiler_params=pltpu.CompilerParams(
            dimension_semantics=("arbitrary", "arbitrary"), vmem_limit_bytes=VMEM_LIMIT),
        name="fox_in",
    )(x, g, wqkv, wf, bf, hsum)


def _attn_block_counts(stats):
    B, nt = stats.shape[0], stats.shape[1]
    per_head = stats[:, :, :4, ::AUG_LANES]
    qn2, kn2, c_minus_diag, c_last = (per_head[:, :, r] for r in range(4))
    kn2_max = jnp.max(kn2, axis=1, keepdims=True)
    bound = jnp.sqrt(qn2 * kn2_max) + c_minus_diag + 1.0
    gap = bound[:, :, None, :] - c_last[:, None, :, :]
    kk = lax.broadcasted_iota(jnp.int32, (nt, nt), 1)
    ii = lax.broadcasted_iota(jnp.int32, (nt, nt), 0)
    need = (gap > -EXP2_ZERO_GAP) & (kk < ii)[None, :, :, None]
    oldest = jnp.min(jnp.where(need, kk[None, :, :, None], nt), axis=2)
    count = jnp.maximum(jnp.arange(nt)[None, :, None] - oldest, 0) + 1
    pairs = jnp.max(count.reshape(B, nt, -1, 2), axis=-1)
    return jnp.transpose(pairs, (0, 2, 1)).reshape(-1).astype(jnp.int32)


def _fox_attn_kernel(nblk_ref, q_ref, *refs, t):
    j = pl.program_id(1)
    nq = q_ref.shape[1] // t

    def tile(i, carry):
        n = nblk_ref[(pl.program_id(0) * pl.num_programs(1) + j) * nq + i]
        _fox_attn_tile(i, n, j, q_ref, *refs, t=t)
        return carry

    lax.fori_loop(0, nq, tile, 0)


def _fox_attn_tile(i, n, j, q_ref, augq_ref, k_ref, v_ref, augk_ref, o_ref, lhs_ref, m_ref,
                   acc_ref, sa_ref, sb_ref, pa_ref, pb_ref, ala_ref, alb_ref, mxa_ref, mxb_ref,
                   *, t):
    lane = lax.broadcasted_iota(jnp.int32, (t, LANES), 1)
    q_rows = pl.ds(pl.multiple_of(i * t, t), t)
    q2 = q_ref[0, q_rows, :]
    aq = augq_ref[0, q_rows, :]
    zero = jnp.zeros((), BF16)
    for hh in range(2):
        lo = (2 * j + hh) * AUG_LANES
        qm = jnp.where((lane >= FOX_HEAD_DIM * hh) & (lane < FOX_HEAD_DIM * (hh + 1)), q2, zero)
        am = jnp.where((lane >= lo) & (lane < lo + AUG_LANES), aq, zero)
        lhs_ref[hh] = jnp.concatenate([qm, am], axis=1)
        m_ref[hh] = jnp.full((t, LANES), NEG_BIG, F32)
        acc_ref[hh] = jnp.zeros((t, LANES), F32)

    one = jnp.ones((), BF16)
    row = lax.broadcasted_iota(jnp.int32, (t, t), 0)
    col = lax.broadcasted_iota(jnp.int32, (t, t), 1)

    def scores(jb, buf, masked=False):
        s_ref, mx_ref, _, _ = buf
        off = pl.multiple_of((i - jb) * t, t)
        rhs = jnp.concatenate([k_ref[0, pl.ds(off, t), :], augk_ref[0, pl.ds(off, t), :]], axis=1)
        for hh in range(2):
            s = lax.dot_general(lhs_ref[hh], rhs, (((1,), (1,)), ((), ())),
                                preferred_element_type=F32)
            if masked:
                s = jnp.where(col <= row, s, NEG_BIG)
            s_ref[hh] = s
            mx = s[:, :LANES]
            for c in range(1, t // LANES):
                mx = jnp.maximum(mx, s[:, c * LANES:(c + 1) * LANES])
            mx_ref[hh] = mx

    def softmax(buf):
        s_ref, mx_ref, p_ref, al_ref = buf
        for hh in range(2):
            m_prev = m_ref[hh]
            m_new = jnp.maximum(m_prev, jnp.max(mx_ref[hh], axis=1, keepdims=True))
            al_ref[hh] = jnp.exp2(m_prev - m_new)
            p_ref[hh] = jnp.exp2(s_ref[hh] - _tile_lanes(m_new, t // LANES)).astype(BF16)
            m_ref[hh] = m_new

    def values(jb, buf):
        _, _, p_ref, al_ref = buf
        off = pl.multiple_of((i - jb) * t, t)
        v2 = v_ref[0, pl.ds(off, t), :]
        for hh in range(2):
            own = (lane >= FOX_HEAD_DIM * hh) & (lane < FOX_HEAD_DIM * (hh + 1))
            pv = jnp.dot(p_ref[hh], jnp.where(own, v2, one), preferred_element_type=F32)
            acc_ref[hh] = al_ref[hh] * acc_ref[hh] + pv

    n_steady = jnp.maximum(n - 2, 0)

    buf_a = (sa_ref, mxa_ref, pa_ref, ala_ref)
    buf_b = (sb_ref, mxb_ref, pb_ref, alb_ref)

    @pl.when(n == 1)
    def _():
        scores(0, buf_a, masked=True)
        softmax(buf_a)
        values(0, buf_a)

    @pl.when(n > 1)
    def _():
        scores(0, buf_a, masked=True)
        softmax(buf_a)
        scores(1, buf_b)

    def step_pair(u, carry):
        tau = 2 + 2 * u
        values(tau - 2, buf_a)
        softmax(buf_b)
        scores(tau, buf_a)
        values(tau - 1, buf_b)
        softmax(buf_a)
        scores(tau + 1, buf_b)
        return carry

    pairs = n_steady // 2
    lax.fori_loop(0, jnp.maximum(pairs - 1, 0), step_pair, 0)

    def drain_even():
        values(n - 2, buf_a)
        softmax(buf_b)
        values(n - 1, buf_b)

    def drain_odd():
        values(n - 3, buf_a)
        softmax(buf_b)
        scores(n - 1, buf_a)
        values(n - 2, buf_b)
        softmax(buf_a)
        values(n - 1, buf_a)

    for parity, drain in ((0, drain_even), (1, drain_odd)):
        @pl.when((n % 2 == parity) & (n >= 2) & (pairs >= 1))
        def _():
            step_pair(pairs - 1, 0)
            drain()

        @pl.when((n % 2 == parity) & (n >= 2) & (pairs == 0))
        def _():
            drain()

    out_a = acc_ref[0] / pltpu.roll(acc_ref[0], FOX_HEAD_DIM, axis=1)
    out_b = acc_ref[1] / pltpu.roll(acc_ref[1], FOX_HEAD_DIM, axis=1)
    o_ref[0, q_rows, :] = jnp.where(lane < FOX_HEAD_DIM, out_a, out_b).astype(BF16)


def _fox_attn(nblk, qkv, augq, augk, *, t):
    B, S, D3 = qkv.shape
    D = D3 // 3
    nb = D // LANES
    grid_spec = pltpu.PrefetchScalarGridSpec(
        num_scalar_prefetch=1,
        grid=(B, nb),
        in_specs=[
            pl.BlockSpec((1, S, LANES), lambda b, j, nblk: (b, 0, j)),
            pl.BlockSpec((1, S, LANES), lambda b, j, nblk: (b, 0, 0)),
            pl.BlockSpec((1, S, LANES), lambda b, j, nblk: (b, 0, nb + j)),
            pl.BlockSpec((1, S, LANES), lambda b, j, nblk: (b, 0, 2 * nb + j)),
            pl.BlockSpec((1, S, LANES), lambda b, j, nblk: (b, 0, 0)),
        ],
        out_specs=pl.BlockSpec((1, S, LANES), lambda b, j, nblk: (b, 0, j)),
        scratch_shapes=[
            pltpu.VMEM((2, t, 2 * LANES), BF16),
            pltpu.VMEM((2, t, LANES), F32),
            pltpu.VMEM((2, t, LANES), F32),
            pltpu.VMEM((2, t, t), F32),
            pltpu.VMEM((2, t, t), F32),
            pltpu.VMEM((2, t, t), BF16),
            pltpu.VMEM((2, t, t), BF16),
            pltpu.VMEM((2, t, LANES), F32),
            pltpu.VMEM((2, t, LANES), F32),
            pltpu.VMEM((2, t, LANES), F32),
            pltpu.VMEM((2, t, LANES), F32),
        ],
    )
    return pl.pallas_call(
        functools.partial(_fox_attn_kernel, t=t),
        grid_spec=grid_spec,
        out_shape=jax.ShapeDtypeStruct((B, S, D), BF16),
        compiler_params=pltpu.CompilerParams(
            dimension_semantics=("arbitrary", "arbitrary"), vmem_limit_bytes=VMEM_LIMIT),
        name="fox_attn",
    )(nblk, qkv, augq, qkv, qkv, augk)


def _ffn_kernel(x_ref, o_ref, wout_ref, g_ref, wup_ref, taps_ref, wd_ref, fg_ref,
                out_ref, tail_ref, x1_ref, h_ref, acc_ref, u_ref, z_ref, *, final):
    ts = x_ref.shape[1]
    ffn_dim = wd_ref.shape[0]
    n_chunks = ffn_dim // FFN_CHUNK

    def cols(c, stream=0):
        lo = stream * ffn_dim + c * FFN_CHUNK
        return slice(lo, lo + FFN_CHUNK)

    @pl.when(pl.program_id(1) == 0)
    def _():
        tail_ref[...] = jnp.zeros_like(tail_ref)

    x1 = x_ref[0] + jnp.dot(o_ref[0], wout_ref[...], preferred_element_type=F32)
    x1_ref[...] = x1
    h_ref[...] = _rmsnorm(x1, g_ref[...]).astype(BF16)
    acc_ref[...] = jnp.zeros_like(acc_ref)
    def up(c):
        h = h_ref[...]
        for stream in range(2):
            u = jnp.dot(h, wup_ref[:, cols(c, stream)], preferred_element_type=F32)
            u_ref[c % FFN_STAGES, stream, 0:8, :] = tail_ref[stream, c]
            u_ref[c % FFN_STAGES, stream, 8:, :] = u
            tail_ref[stream, c] = u[ts - 8:, :]

    def conv(c, stream, cw):
        u0 = u_ref[c % FFN_STAGES, stream, 8:8 + ts, :]
        u1 = u_ref[c % FFN_STAGES, stream, 7:7 + ts, :]
        u2 = u_ref[c % FFN_STAGES, stream, 6:6 + ts, :]
        return cw[3:4, :] + cw[0:1, :] * u2 + cw[1:2, :] * u1 + cw[2:3, :] * u0

    def act(c):
        gate = conv(c, 0, taps_ref[:, cols(c, 0)])
        val = conv(c, 1, taps_ref[:, cols(c, 1)])
        z_ref[c % FFN_STAGES] = (_silu(gate) * val).astype(BF16)

    def down(c):
        acc_ref[...] += jnp.dot(z_ref[c % FFN_STAGES], wd_ref[cols(c), :],
                                preferred_element_type=F32)

    for c in range(min(FFN_UP_LEAD, n_chunks)):
        up(c)
    for c in range(min(FFN_UP_LEAD - 1, n_chunks)):
        act(c)
    for c in range(n_chunks):
        if c + FFN_UP_LEAD < n_chunks:
            up(c + FFN_UP_LEAD)
        if c + FFN_UP_LEAD - 1 < n_chunks:
            act(c + FFN_UP_LEAD - 1)
        down(c)
    y = x1_ref[...] + acc_ref[...]
    if final:
        y = _rmsnorm(y, fg_ref[...])
    out_ref[0] = y


def _ffn(x, o, wout, g, wup, taps, wd, fg, *, ts, final):
    B, S, D = x.shape
    n_chunks = wd.shape[0] // FFN_CHUNK
    c2 = lambda b, s: (0, 0)
    return pl.pallas_call(
        functools.partial(_ffn_kernel, final=final),
        grid=(B, S // ts),
        in_specs=[
            pl.BlockSpec((1, ts, D), lambda b, s: (b, s, 0)),
            pl.BlockSpec((1, ts, D), lambda b, s: (b, s, 0)),
            pl.BlockSpec((D, D), c2),
            pl.BlockSpec((1, D), c2),
            pl.BlockSpec(wup.shape, c2),
            pl.BlockSpec(taps.shape, c2),
            pl.BlockSpec(wd.shape, c2),
            pl.BlockSpec((1, D), c2),
        ],
        out_specs=pl.BlockSpec((1, ts, D), lambda b, s: (b, s, 0)),
        out_shape=jax.ShapeDtypeStruct((B, S, D), F32),
        scratch_shapes=[
            pltpu.VMEM((2, n_chunks, 8, FFN_CHUNK), F32),
            pltpu.VMEM((ts, D), F32),
            pltpu.VMEM((ts, D), BF16),
            pltpu.VMEM((ts, D), F32),
            pltpu.VMEM((FFN_STAGES, 2, 8 + ts, FFN_CHUNK), F32),
            pltpu.VMEM((FFN_STAGES, ts, FFN_CHUNK), BF16),
        ],
        compiler_params=pltpu.CompilerParams(
            dimension_semantics=("arbitrary", "arbitrary"), vmem_limit_bytes=VMEM_LIMIT),
        name="ffn_final" if final else "ffn",
    )(x, o, wout, g, wup, taps, wd, fg)


def _ffn_weights(w_up, conv_w, conv_b, w_down):
    taps = jnp.concatenate(
        [conv_w, conv_b[None, :], jnp.zeros((8 - CONV_WIDTH - 1, conv_w.shape[1]), F32)], axis=0)
    return w_up.astype(BF16), taps, w_down.astype(BF16)


def _hgrn_in_kernel(x_ref, g_ref, w_ref, lb_ref, q_ref, lf_ref, k_ref, i_ref, sg_ref, *, d_model):
    h = _rmsnorm(x_ref[0], g_ref[...]).astype(BF16)
    D = d_model
    q = jnp.dot(h, w_ref[:, 0:D], preferred_element_type=F32)
    q_ref[0] = _silu(q).astype(BF16)
    fl = jnp.dot(h, w_ref[:, D:2 * D], preferred_element_type=F32)
    lb = lb_ref[...]
    f = lb + (1.0 - lb) * (1.0 / (1.0 + jnp.exp(-fl)))
    lf_ref[0] = jnp.log(f)
    k_ref[0] = (1.0 - f).astype(BF16)
    i_ref[0] = jnp.dot(h, w_ref[:, 2 * D:3 * D], preferred_element_type=F32).astype(BF16)
    gg = jnp.dot(h, w_ref[:, 3 * D:4 * D], preferred_element_type=F32)
    sg_ref[0] = _silu(gg).astype(BF16)


def _hgrn_in(x, g, w, lb, *, ts):
    B, S, D = x.shape
    const = lambda b, s: (0, 0)
    tile = pl.BlockSpec((1, ts, D), lambda b, s: (b, s, 0))
    return pl.pallas_call(
        functools.partial(_hgrn_in_kernel, d_model=D),
        grid=(B, S // ts),
        in_specs=[tile, pl.BlockSpec((1, D), const), pl.BlockSpec((D, 4 * D), const),
                  pl.BlockSpec((1, D), const)],
        out_specs=[tile] * 5,
        out_shape=[
            jax.ShapeDtypeStruct((B, S, D), BF16),
            jax.ShapeDtypeStruct((B, S, D), F32),
            jax.ShapeDtypeStruct((B, S, D), BF16),
            jax.ShapeDtypeStruct((B, S, D), BF16),
            jax.ShapeDtypeStruct((B, S, D), BF16),
        ],
        compiler_params=pltpu.CompilerParams(
            dimension_semantics=("arbitrary", "arbitrary"), vmem_limit_bytes=VMEM_LIMIT),
        name="hgrn_in",
    )(x, g, w, lb)


def _chunk_cumsum(x):
    r = lax.broadcasted_iota(jnp.int32, x.shape, 0) & (HGRN_CHUNK - 1)
    k = 1
    while k < HGRN_CHUNK:
        x = x + jnp.where(r >= k, pltpu.roll(x, k, axis=0), 0.0)
        k *= 2
    return x


def _hgrn_core_kernel(q_ref, *refs, ts):
    def tile(s, st):
        return _hgrn_core_tile(s, st, q_ref, *refs, ts=ts)

    lax.fori_loop(0, q_ref.shape[1] // ts, tile,
                  jnp.zeros((HGRN_HEAD_DIM, HGRN_HEAD_DIM), F32))


def _hgrn_core_tile(s, st, q_ref, lf_ref, k_ref, i_ref, sg_ref, og_ref, o_ref, *, ts):
    C = HGRN_CHUNK
    r0 = pl.multiple_of(s * ts, ts)
    rows = pl.ds(r0, ts)
    b_all = _chunk_cumsum(lf_ref[0, rows, :])
    q_all = q_ref[0, rows, :].astype(F32)
    k_all = k_ref[0, rows, :].astype(F32)
    q_dec_all = (q_all * jnp.exp(b_all)).astype(BF16)
    k_inv_all = (k_all * jnp.exp(-b_all)).astype(BF16)
    iv_all = i_ref[0, rows, :]
    crow = lax.broadcasted_iota(jnp.int32, (C, C), 0)
    ccol = lax.broadcasted_iota(jnp.int32, (C, C), 1)

    n_chunks = ts // C
    sls = [slice(n * C, (n + 1) * C) for n in range(n_chunks)]
    b_lasts = [b_all[(n + 1) * C - 1:(n + 1) * C, :] for n in range(n_chunks)]
    a_list, ut_list = [], []
    for n, sl in enumerate(sls):
        a_list.append(lax.dot_general(q_dec_all[sl, :], k_inv_all[sl, :], (((1,), (1,)), ((), ())),
                                      preferred_element_type=F32))
        k_state = (k_all[sl, :] * jnp.exp(b_lasts[n] - b_all[sl, :])).astype(BF16)
        ut_list.append(lax.dot_general(iv_all[sl, :], k_state, (((0,), (0,)), ((), ())),
                                       preferred_element_type=F32))
    st_list = []
    for n in range(n_chunks):
        st_list.append(st.astype(BF16))
        st = jnp.exp(b_lasts[n]) * st + ut_list[n]
    outs = []
    for n, sl in enumerate(sls):
        a = jnp.where(ccol <= crow, a_list[n], 0.0).astype(BF16)
        o = jnp.dot(a, iv_all[sl, :], preferred_element_type=F32)
        outs.append(o + lax.dot_general(q_dec_all[sl, :], st_list[n], (((1,), (1,)), ((), ())),
                                        preferred_element_type=F32))
    o = jnp.concatenate(outs, axis=0)
    o = o * lax.rsqrt(jnp.mean(o * o, axis=-1, keepdims=True) + RMS_EPS)
    o = o * og_ref[...] * sg_ref[0, rows, :].astype(F32)
    o_ref[0, rows, :] = o.astype(BF16)
    return st


def _hgrn_core(q, lf, k, iv, sg, og, *, ts):
    B, S, D = q.shape
    H = D // HGRN_HEAD_DIM
    seq = pl.BlockSpec((1, S, HGRN_HEAD_DIM), lambda b, h: (b, 0, h))
    return pl.pallas_call(
        functools.partial(_hgrn_core_kernel, ts=ts),
        grid=(B, H),
        in_specs=[seq] * 5 + [pl.BlockSpec((1, HGRN_HEAD_DIM), lambda b, h: (0, h))],
        out_specs=seq,
        out_shape=jax.ShapeDtypeStruct((B, S, D), BF16),
        compiler_params=pltpu.CompilerParams(
            dimension_semantics=("arbitrary", "arbitrary"), vmem_limit_bytes=VMEM_LIMIT),
        name="hgrn_core",
    )(q, lf, k, iv, sg, og)


def kernel(x, att_norm_g, att_w_in, att_b_f, att_w_out, hgrn_norm_g, hgrn_w_in, hgrn_lb_logits,
           hgrn_onorm_g, hgrn_w_out, ffn_norm_g, ffn_w_up, ffn_conv_w, ffn_conv_b, ffn_w_down,
           final_norm_g):
    B, S, D = x.shape
    depth = ffn_norm_g.shape[0]
    assert D == FOX_HEADS * FOX_HEAD_DIM and FOX_HEADS * AUG_LANES == LANES
    assert S % ROW_TILE == 0 and S % ATTN_TILE == 0
    assert S % HGRN_TILE == 0 and HGRN_TILE % HGRN_CHUNK == 0
    assert (ffn_w_up.shape[2] // 2) % FFN_CHUNK == 0

    sm = jax.nn.softmax(hgrn_lb_logits.astype(F32), axis=0)
    lower_bounds = jnp.cumsum(sm, axis=0) - sm[0:1]
    fg = final_norm_g.reshape(1, D)
    hsum = (jnp.arange(D)[:, None] // FOX_HEAD_DIM
            == jnp.arange(LANES)[None, :] // AUG_LANES).astype(BF16)

    for layer in range(depth):
        j = layer // 2
        if layer % 2 == 0:
            order = jnp.argsort(att_b_f[j])
            cols = (order[:, None] * FOX_HEAD_DIM + jnp.arange(FOX_HEAD_DIM)[None, :]).reshape(-1)
            w_in = att_w_in[j]
            wqkv = jnp.take(w_in[:, :3 * D].reshape(D, 3, FOX_HEADS, FOX_HEAD_DIM), order,
                            axis=2).reshape(D, 3 * D).astype(BF16)
            wf = jnp.repeat(jnp.take(w_in[:, 3 * D:], order, axis=1), AUG_LANES,
                            axis=1).astype(BF16)
            bf = jnp.repeat(jnp.take(att_b_f[j], order), AUG_LANES).reshape(1, LANES)
            qkv, augq, augk, stats = _fox_in(x, att_norm_g[j].reshape(1, D), wqkv, wf, bf, hsum,
                                             ts=ATTN_TILE)
            o = _fox_attn(_attn_block_counts(stats), qkv, augq, augk, t=ATTN_TILE)
            w_out = jnp.take(att_w_out[j], cols, axis=0)
        else:
            q, lf, k, iv, sg = _hgrn_in(x, hgrn_norm_g[j].reshape(1, D), hgrn_w_in[j].astype(BF16),
                                        lower_bounds[layer].reshape(1, D), ts=HGRN_TILE)
            o = _hgrn_core(q, lf, k, iv, sg, hgrn_onorm_g[j].reshape(1, D), ts=HGRN_TILE)
            w_out = hgrn_w_out[j]
        wup, taps, wd = _ffn_weights(ffn_w_up[layer], ffn_conv_w[layer], ffn_conv_b[layer],
                                     ffn_w_down[layer])
        x = _ffn(x, o, w_out.astype(BF16), ffn_norm_g[layer].reshape(1, D), wup, taps, wd, fg,
                 ts=ROW_TILE, final=(layer == depth - 1))
    return x
```

```python
import functools
import math

import jax
import jax.numpy as jnp
from jax import lax
from jax.experimental import pallas as pl
from jax.experimental.pallas import tpu as pltpu

F32 = jnp.float32
BF16 = jnp.bfloat16

RMS_EPS = 1e-6
FOX_HEADS = 16
FOX_HEAD_DIM = 64
HGRN_HEAD_DIM = 128
HGRN_CHUNK = 64
CONV_WIDTH = 3
LOG2E = math.log2(math.e)

LANES = 128
AUG_LANES = 8
NEG_BIG = -1e30
NORM_SLACK = 1.01
EXP2_ZERO_GAP = 152.0
VMEM_LIMIT = 56 * 1024 * 1024

ROW_TILE = 512
ATTN_TILE = 512
HGRN_TILE = 1024
FFN_CHUNK = 256
FFN_UP_LEAD = 3
FFN_STAGES = FFN_UP_LEAD + 1


def _rmsnorm(x, g):
    ms = jnp.mean(x * x, axis=-1, keepdims=True)
    return x * lax.rsqrt(ms + RMS_EPS) * g


def _split3(x):
    hi = x.astype(BF16)
    r = x - hi.astype(F32)
    mid = r.astype(BF16)
    lo = (r - mid.astype(F32)).astype(BF16)
    return hi, mid, lo


def _silu(x):
    return x * (1.0 / (1.0 + jnp.exp(-x)))


def _tile_lanes(x, n):
    return jnp.concatenate([x] * n, axis=1) if n > 1 else x


def _fox_in_kernel(x_ref, g_ref, wqkv_ref, wf_ref, bf_ref, hsum_ref, qkv_ref, augq_ref, augk_ref,
                   stats_ref, carry_ref, *, d_model, q_scale):
    ts = x_ref.shape[1]

    @pl.when(pl.program_id(1) == 0)
    def _():
        carry_ref[...] = jnp.zeros_like(carry_ref)

    h = _rmsnorm(x_ref[0], g_ref[...]).astype(BF16)

    z = jnp.dot(h, wf_ref[...], preferred_element_type=F32) + bf_ref[...]
    logf = jnp.minimum(z, 0.0) - jnp.log1p(jnp.exp(-jnp.abs(z)))
    lc = logf * LOG2E
    row = lax.broadcasted_iota(jnp.int32, (ts, ts), 0)
    col = lax.broadcasted_iota(jnp.int32, (ts, ts), 1)
    tri = jnp.where(col <= row, 1.0, 0.0).astype(BF16)
    hi, mid, lo = _split3(lc)
    cs = (jnp.dot(tri, hi, preferred_element_type=F32)
          + jnp.dot(tri, mid, preferred_element_type=F32)
          + jnp.dot(tri, lo, preferred_element_type=F32)) + carry_ref[...]
    carry_ref[...] = cs[ts - 1:ts, :]

    c_hi, c_mid, c_lo = (t.astype(F32) for t in _split3(cs))
    j = lax.broadcasted_iota(jnp.int32, (ts, LANES), 1) & (AUG_LANES - 1)
    augq = jnp.where(j == 0, c_hi, jnp.where(j == 1, c_mid, jnp.where(j == 2, c_lo,
                     jnp.where(j < 6, 1.0, 0.0))))
    augk = jnp.where(j < 3, 1.0, jnp.where(j == 3, -c_hi, jnp.where(j == 4, -c_mid,
                     jnp.where(j == 5, -c_lo, 0.0))))
    augq_ref[0] = augq.astype(BF16)
    augk_ref[0] = augk.astype(BF16)

    def project(c):
        acc = jnp.dot(h, wqkv_ref[:, c * d_model:(c + 1) * d_model], preferred_element_type=F32)
        if c == 0:
            acc = acc * q_scale
        yb = acc.astype(BF16)
        qkv_ref[0, :, c * d_model:(c + 1) * d_model] = yb
        return yb.astype(F32)

    stored = [project(0), project(1)]
    norm2 = [jnp.dot(jnp.square(y).astype(BF16), hsum_ref[...], preferred_element_type=F32)
             * NORM_SLACK for y in stored]
    diag = jnp.dot((stored[0] * stored[1]).astype(BF16), hsum_ref[...], preferred_element_type=F32)
    diag_lb = diag - jnp.sqrt(norm2[0] * norm2[1]) * (2.0 ** -8)
    project(2)

    stats_ref[0, 0] = jnp.concatenate(
        [jnp.max(norm2[0], axis=0, keepdims=True), jnp.max(norm2[1], axis=0, keepdims=True),
         jnp.max(cs - diag_lb, axis=0, keepdims=True), cs[ts - 1:ts, :],
         jnp.zeros((4, LANES), F32)], axis=0)


def _fox_in(x, g, wqkv, wf, bf, hsum, *, ts):
    B, S, D = x.shape
    q_scale = LOG2E / math.sqrt(FOX_HEAD_DIM)
    const = lambda b, s: (0, 0)
    return pl.pallas_call(
        functools.partial(_fox_in_kernel, d_model=D, q_scale=q_scale),
        grid=(B, S // ts),
        in_specs=[
            pl.BlockSpec((1, ts, D), lambda b, s: (b, s, 0)),
            pl.BlockSpec((1, D), const),
            pl.BlockSpec((D, 3 * D), const),
            pl.BlockSpec((D, LANES), const),
            pl.BlockSpec((1, LANES), const),
            pl.BlockSpec((D, LANES), const),
        ],
        out_specs=[
            pl.BlockSpec((1, ts, 3 * D), lambda b, s: (b, s, 0)),
            pl.BlockSpec((1, ts, LANES), lambda b, s: (b, s, 0)),
            pl.BlockSpec((1, ts, LANES), lambda b, s: (b, s, 0)),
            pl.BlockSpec((1, 1, 8, LANES), lambda b, s: (b, s, 0, 0)),
        ],
        out_shape=[
            jax.ShapeDtypeStruct((B, S, 3 * D), BF16),
            jax.ShapeDtypeStruct((B, S, LANES), BF16),
            jax.ShapeDtypeStruct((B, S, LANES), BF16),
            jax.ShapeDtypeStruct((B, S // ts, 8, LANES), F32),
        ],
        scratch_shapes=[pltpu.VMEM((1, LANES), F32)],
        compiler_params=pltpu.CompilerParams(
            dimension_semantics=("arbitrary", "arbitrary"), vmem_limit_bytes=VMEM_LIMIT),
        name="fox_in",
    )(x, g, wqkv, wf, bf, hsum)


def _attn_block_counts(stats):
    B, nt = stats.shape[0], stats.shape[1]
    per_head = stats[:, :, :4, ::AUG_LANES]
    qn2, kn2, c_minus_diag, c_last = (per_head[:, :, r] for r in range(4))
    kn2_max = jnp.max(kn2, axis=1, keepdims=True)
    bound = jnp.sqrt(qn2 * kn2_max) + c_minus_diag + 1.0
    gap = bound[:, :, None, :] - c_last[:, None, :, :]
    kk = lax.broadcasted_iota(jnp.int32, (nt, nt), 1)
    ii = lax.broadcasted_iota(jnp.int32, (nt, nt), 0)
    need = (gap > -EXP2_ZERO_GAP) & (kk < ii)[None, :, :, None]
    oldest = jnp.min(jnp.where(need, kk[None, :, :, None], nt), axis=2)
    count = jnp.maximum(jnp.arange(nt)[None, :, None] - oldest, 0) + 1
    pairs = jnp.max(count.reshape(B, nt, -1, 2), axis=-1)
    return jnp.transpose(pairs, (0, 2, 1)).reshape(-1).astype(jnp.int32)


def _fox_attn_kernel(nblk_ref, q_ref, *refs, t):
    j = pl.program_id(1)
    nq = q_ref.shape[1] // t

    def tile(i, carry):
        n = nblk_ref[(pl.program_id(0) * pl.num_programs(1) + j) * nq + i]
        _fox_attn_tile(i, n, j, q_ref, *refs, t=t)
        return carry

    lax.fori_loop(0, nq, tile, 0)


def _fox_attn_tile(i, n, j, q_ref, augq_ref, k_ref, v_ref, augk_ref, o_ref, lhs_ref, m_ref,
                   acc_ref, sa_ref, sb_ref, pa_ref, pb_ref, ala_ref, alb_ref, mxa_ref, mxb_ref,
                   *, t):
    lane = lax.broadcasted_iota(jnp.int32, (t, LANES), 1)
    q_rows = pl.ds(pl.multiple_of(i * t, t), t)
    q2 = q_ref[0, q_rows, :]
    aq = augq_ref[0, q_rows, :]
    zero = jnp.zeros((), BF16)
    for hh in range(2):
        lo = (2 * j + hh) * AUG_LANES
        qm = jnp.where((lane >= FOX_HEAD_DIM * hh) & (lane < FOX_HEAD_DIM * (hh + 1)), q2, zero)
        am = jnp.where((lane >= lo) & (lane < lo + AUG_LANES), aq, zero)
        lhs_ref[hh] = jnp.concatenate([qm, am], axis=1)
        m_ref[hh] = jnp.full((t, LANES), NEG_BIG, F32)
        acc_ref[hh] = jnp.zeros((t, LANES), F32)

    one = jnp.ones((), BF16)
    row = lax.broadcasted_iota(jnp.int32, (t, t), 0)
    col = lax.broadcasted_iota(jnp.int32, (t, t), 1)

    def scores(jb, buf, masked=False):
        s_ref, mx_ref, _, _ = buf
        off = pl.multiple_of((i - jb) * t, t)
        rhs = jnp.concatenate([k_ref[0, pl.ds(off, t), :], augk_ref[0, pl.ds(off, t), :]], axis=1)
        for hh in range(2):
            s = lax.dot_general(lhs_ref[hh], rhs, (((1,), (1,)), ((), ())),
                                preferred_element_type=F32)
            if masked:
                s = jnp.where(col <= row, s, NEG_BIG)
            s_ref[hh] = s
            mx = s[:, :LANES]
            for c in range(1, t // LANES):
                mx = jnp.maximum(mx, s[:, c * LANES:(c + 1) * LANES])
            mx_ref[hh] = mx

    def softmax(buf):
        s_ref, mx_ref, p_ref, al_ref = buf
        for hh in range(2):
            m_prev = m_ref[hh]
            m_new = jnp.maximum(m_prev, jnp.max(mx_ref[hh], axis=1, keepdims=True))
            al_ref[hh] = jnp.exp2(m_prev - m_new)
            p_ref[hh] = jnp.exp2(s_ref[hh] - _tile_lanes(m_new, t // LANES)).astype(BF16)
            m_ref[hh] = m_new

    def values(jb, buf):
        _, _, p_ref, al_ref = buf
        off = pl.multiple_of((i - jb) * t, t)
        v2 = v_ref[0, pl.ds(off, t), :]
        for hh in range(2):
            own = (lane >= FOX_HEAD_DIM * hh) & (lane < FOX_HEAD_DIM * (hh + 1))
            pv = jnp.dot(p_ref[hh], jnp.where(own, v2, one), preferred_element_type=F32)
            acc_ref[hh] = al_ref[hh] * acc_ref[hh] + pv

    n_steady = jnp.maximum(n - 2, 0)

    buf_a = (sa_ref, mxa_ref, pa_ref, ala_ref)
    buf_b = (sb_ref, mxb_ref, pb_ref, alb_ref)

    @pl.when(n == 1)
    def _():
        scores(0, buf_a, masked=True)
        softmax(buf_a)
        values(0, buf_a)

    @pl.when(n > 1)
    def _():
        scores(0, buf_a, masked=True)
        softmax(buf_a)
        scores(1, buf_b)

    def step_pair(u, carry):
        tau = 2 + 2 * u
        values(tau - 2, buf_a)
        softmax(buf_b)
        scores(tau, buf_a)
        values(tau - 1, buf_b)
        softmax(buf_a)
        scores(tau + 1, buf_b)
        return carry

    pairs = n_steady // 2
    lax.fori_loop(0, jnp.maximum(pairs - 1, 0), step_pair, 0)

    def drain_even():
        values(n - 2, buf_a)
        softmax(buf_b)
        values(n - 1, buf_b)

    def drain_odd():
        values(n - 3, buf_a)
        softmax(buf_b)
        scores(n - 1, buf_a)
        values(n - 2, buf_b)
        softmax(buf_a)
        values(n - 1, buf_a)

    for parity, drain in ((0, drain_even), (1, drain_odd)):
        @pl.when((n % 2 == parity) & (n >= 2) & (pairs >= 1))
        def _():
            step_pair(pairs - 1, 0)
            drain()

        @pl.when((n % 2 == parity) & (n >= 2) & (pairs == 0))
        def _():
            drain()

    out_a = acc_ref[0] / pltpu.roll(acc_ref[0], FOX_HEAD_DIM, axis=1)
    out_b = acc_ref[1] / pltpu.roll(acc_ref[1], FOX_HEAD_DIM, axis=1)
    o_ref[0, q_rows, :] = jnp.where(lane < FOX_HEAD_DIM, out_a, out_b).astype(BF16)


def _fox_attn(nblk, qkv, augq, augk, *, t):
    B, S, D3 = qkv.shape
    D = D3 // 3
    nb = D // LANES
    grid_spec = pltpu.PrefetchScalarGridSpec(
        num_scalar_prefetch=1,
        grid=(B, nb),
        in_specs=[
            pl.BlockSpec((1, S, LANES), lambda b, j, nblk: (b, 0, j)),
            pl.BlockSpec((1, S, LANES), lambda b, j, nblk: (b, 0, 0)),
            pl.BlockSpec((1, S, LANES), lambda b, j, nblk: (b, 0, nb + j)),
            pl.BlockSpec((1, S, LANES), lambda b, j, nblk: (b, 0, 2 * nb + j)),
            pl.BlockSpec((1, S, LANES), lambda b, j, nblk: (b, 0, 0)),
        ],
        out_specs=pl.BlockSpec((1, S, LANES), lambda b, j, nblk: (b, 0, j)),
        scratch_shapes=[
            pltpu.VMEM((2, t, 2 * LANES), BF16),
            pltpu.VMEM((2, t, LANES), F32),
            pltpu.VMEM((2, t, LANES), F32),
            pltpu.VMEM((2, t, t), F32),
            pltpu.VMEM((2, t, t), F32),
            pltpu.VMEM((2, t, t), BF16),
            pltpu.VMEM((2, t, t), BF16),
            pltpu.VMEM((2, t, LANES), F32),
            pltpu.VMEM((2, t, LANES), F32),
            pltpu.VMEM((2, t, LANES), F32),
            pltpu.VMEM((2, t, LANES), F32),
        ],
    )
    return pl.pallas_call(
        functools.partial(_fox_attn_kernel, t=t),
        grid_spec=grid_spec,
        out_shape=jax.ShapeDtypeStruct((B, S, D), BF16),
        compiler_params=pltpu.CompilerParams(
            dimension_semantics=("arbitrary", "arbitrary"), vmem_limit_bytes=VMEM_LIMIT),
        name="fox_attn",
    )(nblk, qkv, augq, qkv, qkv, augk)


def _ffn_kernel(x_ref, o_ref, wout_ref, g_ref, wup_ref, taps_ref, wd_ref, fg_ref,
                out_ref, tail_ref, x1_ref, h_ref, acc_ref, u_ref, z_ref, *, final):
    ts = x_ref.shape[1]
    ffn_dim = wd_ref.shape[0]
    n_chunks = ffn_dim // FFN_CHUNK

    def cols(c, stream=0):
        lo = stream * ffn_dim + c * FFN_CHUNK
        return slice(lo, lo + FFN_CHUNK)

    @pl.when(pl.program_id(1) == 0)
    def _():
        tail_ref[...] = jnp.zeros_like(tail_ref)

    x1 = x_ref[0] + jnp.dot(o_ref[0], wout_ref[...], preferred_element_type=F32)
    x1_ref[...] = x1
    h_ref[...] = _rmsnorm(x1, g_ref[...]).astype(BF16)
    acc_ref[...] = jnp.zeros_like(acc_ref)
    def up(c):
        h = h_ref[...]
        for stream in range(2):
            u = jnp.dot(h, wup_ref[:, cols(c, stream)], preferred_element_type=F32)
            u_ref[c % FFN_STAGES, stream, 0:8, :] = tail_ref[stream, c]
            u_ref[c % FFN_STAGES, stream, 8:, :] = u
            tail_ref[stream, c] = u[ts - 8:, :]

    def conv(c, stream, cw):
        u0 = u_ref[c % FFN_STAGES, stream, 8:8 + ts, :]
        u1 = u_ref[c % FFN_STAGES, stream, 7:7 + ts, :]
        u2 = u_ref[c % FFN_STAGES, stream, 6:6 + ts, :]
        return cw[3:4, :] + cw[0:1, :] * u2 + cw[1:2, :] * u1 + cw[2:3, :] * u0

    def act(c):
        gate = conv(c, 0, taps_ref[:, cols(c, 0)])
        val = conv(c, 1, taps_ref[:, cols(c, 1)])
        z_ref[c % FFN_STAGES] = (_silu(gate) * val).astype(BF16)

    def down(c):
        acc_ref[...] += jnp.dot(z_ref[c % FFN_STAGES], wd_ref[cols(c), :],
                                preferred_element_type=F32)

    for c in range(min(FFN_UP_LEAD, n_chunks)):
        up(c)
    for c in range(min(FFN_UP_LEAD - 1, n_chunks)):
        act(c)
    for c in range(n_chunks):
        if c + FFN_UP_LEAD < n_chunks:
            up(c + FFN_UP_LEAD)
        if c + FFN_UP_LEAD - 1 < n_chunks:
            act(c + FFN_UP_LEAD - 1)
        down(c)
    y = x1_ref[...] + acc_ref[...]
    if final:
        y = _rmsnorm(y, fg_ref[...])
    out_ref[0] = y


def _ffn(x, o, wout, g, wup, taps, wd, fg, *, layer, ts, final):
    B, S, D = x.shape
    n_chunks = wd.shape[1] // FFN_CHUNK
    c2 = lambda b, s: (0, 0)
    of_layer = lambda b, s: (layer, 0, 0)
    return pl.pallas_call(
        functools.partial(_ffn_kernel, final=final),
        grid=(B, S // ts),
        in_specs=[
            pl.BlockSpec((1, ts, D), lambda b, s: (b, s, 0)),
            pl.BlockSpec((1, ts, D), lambda b, s: (b, s, 0)),
            pl.BlockSpec((D, D), c2),
            pl.BlockSpec((1, D), c2),
            pl.BlockSpec((None,) + wup.shape[1:], of_layer),
            pl.BlockSpec((None,) + taps.shape[1:], of_layer),
            pl.BlockSpec((None,) + wd.shape[1:], of_layer),
            pl.BlockSpec((1, D), c2),
        ],
        out_specs=pl.BlockSpec((1, ts, D), lambda b, s: (b, s, 0)),
        out_shape=jax.ShapeDtypeStruct((B, S, D), F32),
        scratch_shapes=[
            pltpu.VMEM((2, n_chunks, 8, FFN_CHUNK), F32),
            pltpu.VMEM((ts, D), F32),
            pltpu.VMEM((ts, D), BF16),
            pltpu.VMEM((ts, D), F32),
            pltpu.VMEM((FFN_STAGES, 2, 8 + ts, FFN_CHUNK), F32),
            pltpu.VMEM((FFN_STAGES, ts, FFN_CHUNK), BF16),
        ],
        compiler_params=pltpu.CompilerParams(
            dimension_semantics=("arbitrary", "arbitrary"), vmem_limit_bytes=VMEM_LIMIT),
        name="ffn_final" if final else "ffn",
    )(x, o, wout, g, wup, taps, wd, fg)


def _ffn_weights(w_up, conv_w, conv_b, w_down):
    depth, _, f2 = conv_w.shape
    taps = jnp.concatenate(
        [conv_w, conv_b[:, None, :], jnp.zeros((depth, 8 - CONV_WIDTH - 1, f2), F32)], axis=1)
    return w_up.astype(BF16), taps, w_down.astype(BF16)


def _hgrn_in_kernel(x_ref, g_ref, w_ref, lb_ref, q_ref, lf_ref, k_ref, i_ref, sg_ref, *, d_model):
    h = _rmsnorm(x_ref[0], g_ref[...]).astype(BF16)
    D = d_model
    q = jnp.dot(h, w_ref[:, 0:D], preferred_element_type=F32)
    q_ref[0] = _silu(q).astype(BF16)
    fl = jnp.dot(h, w_ref[:, D:2 * D], preferred_element_type=F32)
    lb = lb_ref[...]
    f = lb + (1.0 - lb) * (1.0 / (1.0 + jnp.exp(-fl)))
    lf_ref[0] = jnp.log(f)
    k_ref[0] = (1.0 - f).astype(BF16)
    gg = jnp.dot(h, w_ref[:, 3 * D:4 * D], preferred_element_type=F32)
    sg_ref[0] = _silu(gg).astype(BF16)
    i_ref[0] = jnp.dot(h, w_ref[:, 2 * D:3 * D], preferred_element_type=F32).astype(BF16)


def _hgrn_in(x, g, w, lb, *, ts):
    B, S, D = x.shape
    const = lambda b, s: (0, 0)
    tile = pl.BlockSpec((1, ts, D), lambda b, s: (b, s, 0))
    return pl.pallas_call(
        functools.partial(_hgrn_in_kernel, d_model=D),
        grid=(B, S // ts),
        in_specs=[tile, pl.BlockSpec((1, D), const), pl.BlockSpec((D, 4 * D), const),
                  pl.BlockSpec((1, D), const)],
        out_specs=[tile] * 5,
        out_shape=[
            jax.ShapeDtypeStruct((B, S, D), BF16),
            jax.ShapeDtypeStruct((B, S, D), F32),
            jax.ShapeDtypeStruct((B, S, D), BF16),
            jax.ShapeDtypeStruct((B, S, D), BF16),
            jax.ShapeDtypeStruct((B, S, D), BF16),
        ],
        compiler_params=pltpu.CompilerParams(
            dimension_semantics=("arbitrary", "arbitrary"), vmem_limit_bytes=VMEM_LIMIT),
        name="hgrn_in",
    )(x, g, w, lb)


def _chunk_cumsum(x):
    r = lax.broadcasted_iota(jnp.int32, x.shape, 0) & (HGRN_CHUNK - 1)
    k = 1
    while k < HGRN_CHUNK:
        x = x + jnp.where(r >= k, pltpu.roll(x, k, axis=0), 0.0)
        k *= 2
    return x


def _hgrn_core_kernel(q_ref, *refs, ts):
    def tile_pair(it, st):
        st = _hgrn_core_tile(2 * it, st, q_ref, *refs, ts=ts)
        return _hgrn_core_tile(2 * it + 1, st, q_ref, *refs, ts=ts)

    lax.fori_loop(0, q_ref.shape[1] // (2 * ts), tile_pair,
                  jnp.zeros((HGRN_HEAD_DIM, HGRN_HEAD_DIM), F32))


def _hgrn_core_tile(s, st, q_ref, lf_ref, k_ref, i_ref, sg_ref, og_ref, o_ref, *, ts):
    C = HGRN_CHUNK
    r0 = pl.multiple_of(s * ts, ts)
    rows = pl.ds(r0, ts)
    b_all = _chunk_cumsum(lf_ref[0, rows, :])
    q_all = q_ref[0, rows, :].astype(F32)
    k_all = k_ref[0, rows, :].astype(F32)
    q_dec_all = (q_all * jnp.exp(b_all)).astype(BF16)
    k_inv_all = (k_all * jnp.exp(-b_all)).astype(BF16)
    iv_all = i_ref[0, rows, :]
    crow = lax.broadcasted_iota(jnp.int32, (C, C), 0)
    ccol = lax.broadcasted_iota(jnp.int32, (C, C), 1)

    n_chunks = ts // C
    sls = [slice(n * C, (n + 1) * C) for n in range(n_chunks)]
    b_lasts = [b_all[(n + 1) * C - 1:(n + 1) * C, :] for n in range(n_chunks)]
    a_list, ut_list = [], []
    for n, sl in enumerate(sls):
        a_list.append(lax.dot_general(q_dec_all[sl, :], k_inv_all[sl, :], (((1,), (1,)), ((), ())),
                                      preferred_element_type=F32))
        k_state = (k_all[sl, :] * jnp.exp(b_lasts[n] - b_all[sl, :])).astype(BF16)
        ut_list.append(lax.dot_general(iv_all[sl, :], k_state, (((0,), (0,)), ((), ())),
                                       preferred_element_type=F32))
    st_list = []
    for n in range(n_chunks):
        st_list.append(st.astype(BF16))
        st = jnp.exp(b_lasts[n]) * st + ut_list[n]
    outs = []
    for n, sl in enumerate(sls):
        a = jnp.where(ccol <= crow, a_list[n], 0.0).astype(BF16)
        o = jnp.dot(a, iv_all[sl, :], preferred_element_type=F32)
        outs.append(o + lax.dot_general(q_dec_all[sl, :], st_list[n], (((1,), (1,)), ((), ())),
                                        preferred_element_type=F32))
    o = jnp.concatenate(outs, axis=0)
    o = o * lax.rsqrt(jnp.mean(o * o, axis=-1, keepdims=True) + RMS_EPS)
    o = o * og_ref[...] * sg_ref[0, rows, :].astype(F32)
    o_ref[0, rows, :] = o.astype(BF16)
    return st


def _hgrn_core(q, lf, k, iv, sg, og, *, ts):
    B, S, D = q.shape
    H = D // HGRN_HEAD_DIM
    seq = pl.BlockSpec((1, S, HGRN_HEAD_DIM), lambda b, h: (b, 0, h))
    return pl.pallas_call(
        functools.partial(_hgrn_core_kernel, ts=ts),
        grid=(B, H),
        in_specs=[seq] * 5 + [pl.BlockSpec((1, HGRN_HEAD_DIM), lambda b, h: (0, h))],
        out_specs=seq,
        out_shape=jax.ShapeDtypeStruct((B, S, D), BF16),
        compiler_params=pltpu.CompilerParams(
            dimension_semantics=("arbitrary", "arbitrary"), vmem_limit_bytes=VMEM_LIMIT),
        name="hgrn_core",
    )(q, lf, k, iv, sg, og)


def kernel(x, att_norm_g, att_w_in, att_b_f, att_w_out, hgrn_norm_g, hgrn_w_in, hgrn_lb_logits,
           hgrn_onorm_g, hgrn_w_out, ffn_norm_g, ffn_w_up, ffn_conv_w, ffn_conv_b, ffn_w_down,
           final_norm_g):
    B, S, D = x.shape
    depth = ffn_norm_g.shape[0]
    assert D == FOX_HEADS * FOX_HEAD_DIM and FOX_HEADS * AUG_LANES == LANES
    assert S % ROW_TILE == 0 and S % ATTN_TILE == 0
    assert S % (2 * HGRN_TILE) == 0 and HGRN_TILE % HGRN_CHUNK == 0
    assert (ffn_w_up.shape[2] // 2) % FFN_CHUNK == 0

    sm = jax.nn.softmax(hgrn_lb_logits.astype(F32), axis=0)
    lower_bounds = jnp.cumsum(sm, axis=0) - sm[0:1]
    fg = final_norm_g.reshape(1, D)
    hsum = (jnp.arange(D)[:, None] // FOX_HEAD_DIM
            == jnp.arange(LANES)[None, :] // AUG_LANES).astype(BF16)

    wup, taps, wd = _ffn_weights(ffn_w_up, ffn_conv_w, ffn_conv_b, ffn_w_down)

    for layer in range(depth):
        j = layer // 2
        if layer % 2 == 0:
            order = jnp.argsort(att_b_f[j])
            cols = (order[:, None] * FOX_HEAD_DIM + jnp.arange(FOX_HEAD_DIM)[None, :]).reshape(-1)
            w_in = att_w_in[j]
            wqkv = jnp.take(w_in[:, :3 * D].reshape(D, 3, FOX_HEADS, FOX_HEAD_DIM), order,
                            axis=2).reshape(D, 3 * D).astype(BF16)
            wf = jnp.repeat(jnp.take(w_in[:, 3 * D:], order, axis=1), AUG_LANES,
                            axis=1).astype(BF16)
            bf = jnp.repeat(jnp.take(att_b_f[j], order), AUG_LANES).reshape(1, LANES)
            qkv, augq, augk, stats = _fox_in(x, att_norm_g[j].reshape(1, D), wqkv, wf, bf, hsum,
                                             ts=ATTN_TILE)
            o = _fox_attn(_attn_block_counts(stats), qkv, augq, augk, t=ATTN_TILE)
            w_out = jnp.take(att_w_out[j], cols, axis=0)
        else:
            q, lf, k, iv, sg = _hgrn_in(x, hgrn_norm_g[j].reshape(1, D), hgrn_w_in[j].astype(BF16),
                                        lower_bounds[layer].reshape(1, D), ts=HGRN_TILE)
            o = _hgrn_core(q, lf, k, iv, sg, hgrn_onorm_g[j].reshape(1, D), ts=HGRN_TILE)
            w_out = hgrn_w_out[j]
        x = _ffn(x, o, w_out.astype(BF16), ffn_norm_g[layer].reshape(1, D), wup, taps, wd, fg,
                 layer=layer, ts=ROW_TILE, final=(layer == depth - 1))
    return x
```

```python
import functools
import math

import jax
import jax.numpy as jnp
from jax import lax
from jax.experimental import pallas as pl
from jax.experimental.pallas import tpu as pltpu

F32 = jnp.float32
BF16 = jnp.bfloat16

RMS_EPS = 1e-6
FOX_HEADS = 16
FOX_HEAD_DIM = 64
HGRN_HEAD_DIM = 128
HGRN_CHUNK = 64
CONV_WIDTH = 3
LOG2E = math.log2(math.e)

LANES = 128
AUG_LANES = 8
NEG_BIG = -1e30
NORM_SLACK = 1.01
EXP2_ZERO_GAP = 152.0
VMEM_LIMIT = 56 * 1024 * 1024

ROW_TILE = 512
ATTN_TILE = 512
HGRN_TILE = 1024
FFN_CHUNK = 256
FFN_UP_LEAD = 4
FFN_STAGES = FFN_UP_LEAD + 1


def _rmsnorm(x, g):
    ms = jnp.mean(x * x, axis=-1, keepdims=True)
    return x * lax.rsqrt(ms + RMS_EPS) * g


def _split3(x):
    hi = x.astype(BF16)
    r = x - hi.astype(F32)
    mid = r.astype(BF16)
    lo = (r - mid.astype(F32)).astype(BF16)
    return hi, mid, lo


def _silu(x):
    return x * (1.0 / (1.0 + jnp.exp(-x)))


def _tile_lanes(x, n):
    return jnp.concatenate([x] * n, axis=1) if n > 1 else x


def _fox_in_kernel(x_ref, g_ref, wqkv_ref, wf_ref, bf_ref, hsum_ref, qkv_ref, augq_ref, augk_ref,
                   stats_ref, carry_ref, *, d_model, q_scale):
    ts = x_ref.shape[1]

    @pl.when(pl.program_id(1) == 0)
    def _():
        carry_ref[...] = jnp.zeros_like(carry_ref)

    h = _rmsnorm(x_ref[0], g_ref[...]).astype(BF16)

    z = jnp.dot(h, wf_ref[...], preferred_element_type=F32) + bf_ref[...]
    logf = jnp.minimum(z, 0.0) - jnp.log1p(jnp.exp(-jnp.abs(z)))
    lc = logf * LOG2E
    row = lax.broadcasted_iota(jnp.int32, (ts, ts), 0)
    col = lax.broadcasted_iota(jnp.int32, (ts, ts), 1)
    tri = jnp.where(col <= row, 1.0, 0.0).astype(BF16)
    hi, mid, lo = _split3(lc)
    cs = (jnp.dot(tri, hi, preferred_element_type=F32)
          + jnp.dot(tri, mid, preferred_element_type=F32)
          + jnp.dot(tri, lo, preferred_element_type=F32)) + carry_ref[...]
    carry_ref[...] = cs[ts - 1:ts, :]

    c_hi, c_mid, c_lo = (t.astype(F32) for t in _split3(cs))
    j = lax.broadcasted_iota(jnp.int32, (ts, LANES), 1) & (AUG_LANES - 1)
    augq = jnp.where(j == 0, c_hi, jnp.where(j == 1, c_mid, jnp.where(j == 2, c_lo,
                     jnp.where(j < 6, 1.0, 0.0))))
    augk = jnp.where(j < 3, 1.0, jnp.where(j == 3, -c_hi, jnp.where(j == 4, -c_mid,
                     jnp.where(j == 5, -c_lo, 0.0))))
    augq_ref[0] = augq.astype(BF16)
    augk_ref[0] = augk.astype(BF16)

    def project(c):
        acc = jnp.dot(h, wqkv_ref[:, c * d_model:(c + 1) * d_model], preferred_element_type=F32)
        if c == 0:
            acc = acc * q_scale
        yb = acc.astype(BF16)
        qkv_ref[0, :, c * d_model:(c + 1) * d_model] = yb
        return yb.astype(F32)

    stored = [project(0), project(1)]
    norm2 = [jnp.dot(jnp.square(y).astype(BF16), hsum_ref[...], preferred_element_type=F32)
             * NORM_SLACK for y in stored]
    diag = jnp.dot((stored[0] * stored[1]).astype(BF16), hsum_ref[...], preferred_element_type=F32)
    diag_lb = diag - jnp.sqrt(norm2[0] * norm2[1]) * (2.0 ** -8)
    project(2)

    stats_ref[0, 0] = jnp.concatenate(
        [jnp.max(norm2[0], axis=0, keepdims=True), jnp.max(norm2[1], axis=0, keepdims=True),
         jnp.max(cs - diag_lb, axis=0, keepdims=True), cs[ts - 1:ts, :],
         jnp.zeros((4, LANES), F32)], axis=0)


def _fox_in(x, g, wqkv, wf, bf, hsum, *, ts):
    B, S, D = x.shape
    q_scale = LOG2E / math.sqrt(FOX_HEAD_DIM)
    const = lambda b, s: (0, 0)
    return pl.pallas_call(
        functools.partial(_fox_in_kernel, d_model=D, q_scale=q_scale),
        grid=(B, S // ts),
        in_specs=[
            pl.BlockSpec((1, ts, D), lambda b, s: (b, s, 0)),
            pl.BlockSpec((1, D), const),
            pl.BlockSpec((D, 3 * D), const),
            pl.BlockSpec((D, LANES), const),
            pl.BlockSpec((1, LANES), const),
            pl.BlockSpec((D, LANES), const),
        ],
        out_specs=[
            pl.BlockSpec((1, ts, 3 * D), lambda b, s: (b, s, 0)),
            pl.BlockSpec((1, ts, LANES), lambda b, s: (b, s, 0)),
            pl.BlockSpec((1, ts, LANES), lambda b, s: (b, s, 0)),
            pl.BlockSpec((1, 1, 8, LANES), lambda b, s: (b, s, 0, 0)),
        ],
        out_shape=[
            jax.ShapeDtypeStruct((B, S, 3 * D), BF16),
            jax.ShapeDtypeStruct((B, S, LANES), BF16),
            jax.ShapeDtypeStruct((B, S, LANES), BF16),
            jax.ShapeDtypeStruct((B, S // ts, 8, LANES), F32),
        ],
        scratch_shapes=[pltpu.VMEM((1, LANES), F32)],
        compiler_params=pltpu.CompilerParams(
            dimension_semantics=("arbitrary", "arbitrary"), vmem_limit_bytes=VMEM_LIMIT),
        name="fox_in",
    )(x, g, wqkv, wf, bf, hsum)


def _attn_block_counts(stats):
    B, nt = stats.shape[0], stats.shape[1]
    per_head = stats[:, :, :4, ::AUG_LANES]
    qn2, kn2, c_minus_diag, c_last = (per_head[:, :, r] for r in range(4))
    kn2_max = jnp.max(kn2, axis=1, keepdims=True)
    bound = jnp.sqrt(qn2 * kn2_max) + c_minus_diag + 1.0
    gap = bound[:, :, None, :] - c_last[:, None, :, :]
    kk = lax.broadcasted_iota(jnp.int32, (nt, nt), 1)
    ii = lax.broadcasted_iota(jnp.int32, (nt, nt), 0)
    need = (gap > -EXP2_ZERO_GAP) & (kk < ii)[None, :, :, None]
    oldest = jnp.min(jnp.where(need, kk[None, :, :, None], nt), axis=2)
    count = jnp.maximum(jnp.arange(nt)[None, :, None] - oldest, 0) + 1
    pairs = jnp.max(count.reshape(B, nt, -1, 2), axis=-1)
    return jnp.transpose(pairs, (0, 2, 1)).reshape(-1).astype(jnp.int32)


def _fox_attn_kernel(nblk_ref, q_ref, *refs, t):
    j = pl.program_id(1)
    nq = q_ref.shape[1] // t

    def tile(i, carry):
        n = nblk_ref[(pl.program_id(0) * pl.num_programs(1) + j) * nq + i]
        _fox_attn_tile(i, n, j, q_ref, *refs, t=t)
        return carry

    lax.fori_loop(0, nq, tile, 0)


def _fox_attn_tile(i, n, j, q_ref, augq_ref, k_ref, v_ref, augk_ref, o_ref, lhs_ref, m_ref,
                   acc_ref, sa_ref, sb_ref, pa_ref, pb_ref, ala_ref, alb_ref, mxa_ref, mxb_ref,
                   *, t):
    lane = lax.broadcasted_iota(jnp.int32, (t, LANES), 1)
    q_rows = pl.ds(pl.multiple_of(i * t, t), t)
    q2 = q_ref[0, q_rows, :]
    aq = augq_ref[0, q_rows, :]
    zero = jnp.zeros((), BF16)
    for hh in range(2):
        lo = (2 * j + hh) * AUG_LANES
        qm = jnp.where((lane >= FOX_HEAD_DIM * hh) & (lane < FOX_HEAD_DIM * (hh + 1)), q2, zero)
        am = jnp.where((lane >= lo) & (lane < lo + AUG_LANES), aq, zero)
        lhs_ref[hh] = jnp.concatenate([qm, am], axis=1)
        m_ref[hh] = jnp.full((t, LANES), NEG_BIG, F32)
        acc_ref[hh] = jnp.zeros((t, LANES), F32)

    one = jnp.ones((), BF16)
    row = lax.broadcasted_iota(jnp.int32, (t, t), 0)
    col = lax.broadcasted_iota(jnp.int32, (t, t), 1)

    def scores(jb, buf, masked=False):
        s_ref, mx_ref, _, _ = buf
        off = pl.multiple_of((i - jb) * t, t)
        rhs = jnp.concatenate([k_ref[0, pl.ds(off, t), :], augk_ref[0, pl.ds(off, t), :]], axis=1)
        for hh in range(2):
            s = lax.dot_general(lhs_ref[hh], rhs, (((1,), (1,)), ((), ())),
                                preferred_element_type=F32)
            if masked:
                s = jnp.where(col <= row, s, NEG_BIG)
            s_ref[hh] = s
            mx = s[:, :LANES]
            for c in range(1, t // LANES):
                mx = jnp.maximum(mx, s[:, c * LANES:(c + 1) * LANES])
            mx_ref[hh] = mx

    def softmax(buf):
        s_ref, mx_ref, p_ref, al_ref = buf
        for hh in range(2):
            m_prev = m_ref[hh]
            m_new = jnp.maximum(m_prev, jnp.max(mx_ref[hh], axis=1, keepdims=True))
            al_ref[hh] = jnp.exp2(m_prev - m_new)
            p_ref[hh] = jnp.exp2(s_ref[hh] - _tile_lanes(m_new, t // LANES)).astype(BF16)
            m_ref[hh] = m_new

    def values(jb, buf):
        _, _, p_ref, al_ref = buf
        off = pl.multiple_of((i - jb) * t, t)
        v2 = v_ref[0, pl.ds(off, t), :]
        for hh in range(2):
            own = (lane >= FOX_HEAD_DIM * hh) & (lane < FOX_HEAD_DIM * (hh + 1))
            pv = jnp.dot(p_ref[hh], jnp.where(own, v2, one), preferred_element_type=F32)
            acc_ref[hh] = al_ref[hh] * acc_ref[hh] + pv

    n_steady = jnp.maximum(n - 2, 0)

    buf_a = (sa_ref, mxa_ref, pa_ref, ala_ref)
    buf_b = (sb_ref, mxb_ref, pb_ref, alb_ref)

    @pl.when(n == 1)
    def _():
        scores(0, buf_a, masked=True)
        softmax(buf_a)
        values(0, buf_a)

    @pl.when(n > 1)
    def _():
        scores(0, buf_a, masked=True)
        softmax(buf_a)
        scores(1, buf_b)

    def step_pair(u, carry):
        tau = 2 + 2 * u
        values(tau - 2, buf_a)
        softmax(buf_b)
        scores(tau, buf_a)
        values(tau - 1, buf_b)
        softmax(buf_a)
        scores(tau + 1, buf_b)
        return carry

    pairs = n_steady // 2
    lax.fori_loop(0, jnp.maximum(pairs - 1, 0), step_pair, 0)

    def drain_even():
        values(n - 2, buf_a)
        softmax(buf_b)
        values(n - 1, buf_b)

    def drain_odd():
        values(n - 3, buf_a)
        softmax(buf_b)
        scores(n - 1, buf_a)
        values(n - 2, buf_b)
        softmax(buf_a)
        values(n - 1, buf_a)

    for parity, drain in ((0, drain_even), (1, drain_odd)):
        @pl.when((n % 2 == parity) & (n >= 2) & (pairs >= 1))
        def _():
            step_pair(pairs - 1, 0)
            drain()

        @pl.when((n % 2 == parity) & (n >= 2) & (pairs == 0))
        def _():
            drain()

    out_a = acc_ref[0] / pltpu.roll(acc_ref[0], FOX_HEAD_DIM, axis=1)
    out_b = acc_ref[1] / pltpu.roll(acc_ref[1], FOX_HEAD_DIM, axis=1)
    o_ref[0, q_rows, :] = jnp.where(lane < FOX_HEAD_DIM, out_a, out_b).astype(BF16)


def _fox_attn(nblk, qkv, augq, augk, *, t):
    B, S, D3 = qkv.shape
    D = D3 // 3
    nb = D // LANES
    grid_spec = pltpu.PrefetchScalarGridSpec(
        num_scalar_prefetch=1,
        grid=(B, nb),
        in_specs=[
            pl.BlockSpec((1, S, LANES), lambda b, j, nblk: (b, 0, j)),
            pl.BlockSpec((1, S, LANES), lambda b, j, nblk: (b, 0, 0)),
            pl.BlockSpec((1, S, LANES), lambda b, j, nblk: (b, 0, nb + j)),
            pl.BlockSpec((1, S, LANES), lambda b, j, nblk: (b, 0, 2 * nb + j)),
            pl.BlockSpec((1, S, LANES), lambda b, j, nblk: (b, 0, 0)),
        ],
        out_specs=pl.BlockSpec((1, S, LANES), lambda b, j, nblk: (b, 0, j)),
        scratch_shapes=[
            pltpu.VMEM((2, t, 2 * LANES), BF16),
            pltpu.VMEM((2, t, LANES), F32),
            pltpu.VMEM((2, t, LANES), F32),
            pltpu.VMEM((2, t, t), F32),
            pltpu.VMEM((2, t, t), F32),
            pltpu.VMEM((2, t, t), BF16),
            pltpu.VMEM((2, t, t), BF16),
            pltpu.VMEM((2, t, LANES), F32),
            pltpu.VMEM((2, t, LANES), F32),
            pltpu.VMEM((2, t, LANES), F32),
            pltpu.VMEM((2, t, LANES), F32),
        ],
    )
    return pl.pallas_call(
        functools.partial(_fox_attn_kernel, t=t),
        grid_spec=grid_spec,
        out_shape=jax.ShapeDtypeStruct((B, S, D), BF16),
        compiler_params=pltpu.CompilerParams(
            dimension_semantics=("arbitrary", "arbitrary"), vmem_limit_bytes=VMEM_LIMIT),
        name="fox_attn",
    )(nblk, qkv, augq, qkv, qkv, augk)


def _ffn_kernel(x_ref, o_ref, wout_ref, g_ref, wup_ref, taps_ref, wd_ref, fg_ref,
                out_ref, tail_ref, x1_ref, h_ref, acc_ref, u_ref, z_ref, *, final):
    ts = x_ref.shape[1]
    ffn_dim = wd_ref.shape[0]
    n_chunks = ffn_dim // FFN_CHUNK

    def cols(c, stream=0):
        lo = stream * ffn_dim + c * FFN_CHUNK
        return slice(lo, lo + FFN_CHUNK)

    @pl.when(pl.program_id(1) == 0)
    def _():
        tail_ref[...] = jnp.zeros_like(tail_ref)

    x1 = x_ref[0] + jnp.dot(o_ref[0], wout_ref[...], preferred_element_type=F32)
    x1_ref[...] = x1
    h_ref[...] = _rmsnorm(x1, g_ref[...]).astype(BF16)
    acc_ref[...] = jnp.zeros_like(acc_ref)
    def up(c):
        h = h_ref[...]
        for stream in range(2):
            u = jnp.dot(h, wup_ref[:, cols(c, stream)], preferred_element_type=F32)
            u_ref[c % FFN_STAGES, stream, 0:8, :] = tail_ref[stream, c]
            u_ref[c % FFN_STAGES, stream, 8:, :] = u
            tail_ref[stream, c] = u[ts - 8:, :]

    def conv(c, stream, cw):
        u0 = u_ref[c % FFN_STAGES, stream, 8:8 + ts, :]
        u1 = u_ref[c % FFN_STAGES, stream, 7:7 + ts, :]
        u2 = u_ref[c % FFN_STAGES, stream, 6:6 + ts, :]
        return cw[3:4, :] + cw[0:1, :] * u2 + cw[1:2, :] * u1 + cw[2:3, :] * u0

    def act(c):
        gate = conv(c, 0, taps_ref[:, cols(c, 0)])
        val = conv(c, 1, taps_ref[:, cols(c, 1)])
        z_ref[c % FFN_STAGES] = (_silu(gate) * val).astype(BF16)

    def down(c):
        acc_ref[...] += jnp.dot(z_ref[c % FFN_STAGES], wd_ref[cols(c), :],
                                preferred_element_type=F32)

    for c in range(min(FFN_UP_LEAD, n_chunks)):
        up(c)
    for c in range(min(FFN_UP_LEAD - 1, n_chunks)):
        act(c)
    for c in range(n_chunks):
        if c + FFN_UP_LEAD < n_chunks:
            up(c + FFN_UP_LEAD)
        if c + FFN_UP_LEAD - 1 < n_chunks:
            act(c + FFN_UP_LEAD - 1)
        down(c)
    y = x1_ref[...] + acc_ref[...]
    if final:
        y = _rmsnorm(y, fg_ref[...])
    out_ref[0] = y


def _ffn(x, o, wout, g, wup, taps, wd, fg, *, layer, ts, final):
    B, S, D = x.shape
    n_chunks = wd.shape[1] // FFN_CHUNK
    c2 = lambda b, s: (0, 0)
    of_layer = lambda b, s: (layer, 0, 0)
    return pl.pallas_call(
        functools.partial(_ffn_kernel, final=final),
        grid=(B, S // ts),
        in_specs=[
            pl.BlockSpec((1, ts, D), lambda b, s: (b, s, 0)),
            pl.BlockSpec((1, ts, D), lambda b, s: (b, s, 0)),
            pl.BlockSpec((D, D), c2),
            pl.BlockSpec((1, D), c2),
            pl.BlockSpec((None,) + wup.shape[1:], of_layer),
            pl.BlockSpec((None,) + taps.shape[1:], of_layer),
            pl.BlockSpec((None,) + wd.shape[1:], of_layer),
            pl.BlockSpec((1, D), c2),
        ],
        out_specs=pl.BlockSpec((1, ts, D), lambda b, s: (b, s, 0)),
        out_shape=jax.ShapeDtypeStruct((B, S, D), F32),
        scratch_shapes=[
            pltpu.VMEM((2, n_chunks, 8, FFN_CHUNK), F32),
            pltpu.VMEM((ts, D), F32),
            pltpu.VMEM((ts, D), BF16),
            pltpu.VMEM((ts, D), F32),
            pltpu.VMEM((FFN_STAGES, 2, 8 + ts, FFN_CHUNK), F32),
            pltpu.VMEM((FFN_STAGES, ts, FFN_CHUNK), BF16),
        ],
        compiler_params=pltpu.CompilerParams(
            dimension_semantics=("arbitrary", "arbitrary"), vmem_limit_bytes=VMEM_LIMIT),
        name="ffn_final" if final else "ffn",
    )(x, o, wout, g, wup, taps, wd, fg)


def _ffn_weights(w_up, conv_w, conv_b, w_down):
    depth, _, f2 = conv_w.shape
    taps = jnp.concatenate(
        [conv_w, conv_b[:, None, :], jnp.zeros((depth, 8 - CONV_WIDTH - 1, f2), F32)], axis=1)
    return w_up.astype(BF16), taps, w_down.astype(BF16)


def _hgrn_in_kernel(x_ref, g_ref, w_ref, lb_ref, q_ref, lf_ref, k_ref, i_ref, sg_ref, *, d_model):
    h = _rmsnorm(x_ref[0], g_ref[...]).astype(BF16)
    D = d_model
    q = jnp.dot(h, w_ref[:, 0:D], preferred_element_type=F32)
    q_ref[0] = _silu(q).astype(BF16)
    fl = jnp.dot(h, w_ref[:, D:2 * D], preferred_element_type=F32)
    lb = lb_ref[...]
    f = lb + (1.0 - lb) * (1.0 / (1.0 + jnp.exp(-fl)))
    lf_ref[0] = jnp.log(f)
    k_ref[0] = (1.0 - f).astype(BF16)
    gg = jnp.dot(h, w_ref[:, 3 * D:4 * D], preferred_element_type=F32)
    sg_ref[0] = _silu(gg).astype(BF16)
    i_ref[0] = jnp.dot(h, w_ref[:, 2 * D:3 * D], preferred_element_type=F32).astype(BF16)


def _hgrn_in(x, g, w, lb, *, ts):
    B, S, D = x.shape
    const = lambda b, s: (0, 0)
    tile = pl.BlockSpec((1, ts, D), lambda b, s: (b, s, 0))
    return pl.pallas_call(
        functools.partial(_hgrn_in_kernel, d_model=D),
        grid=(B, S // ts),
        in_specs=[tile, pl.BlockSpec((1, D), const), pl.BlockSpec((D, 4 * D), const),
                  pl.BlockSpec((1, D), const)],
        out_specs=[tile] * 5,
        out_shape=[
            jax.ShapeDtypeStruct((B, S, D), BF16),
            jax.ShapeDtypeStruct((B, S, D), F32),
            jax.ShapeDtypeStruct((B, S, D), BF16),
            jax.ShapeDtypeStruct((B, S, D), BF16),
            jax.ShapeDtypeStruct((B, S, D), BF16),
        ],
        compiler_params=pltpu.CompilerParams(
            dimension_semantics=("arbitrary", "arbitrary"), vmem_limit_bytes=VMEM_LIMIT),
        name="hgrn_in",
    )(x, g, w, lb)


def _chunk_cumsum(x):
    r = lax.broadcasted_iota(jnp.int32, x.shape, 0) & (HGRN_CHUNK - 1)
    k = 1
    while k < HGRN_CHUNK:
        x = x + jnp.where(r >= k, pltpu.roll(x, k, axis=0), 0.0)
        k *= 2
    return x


def _hgrn_core_kernel(q_ref, *refs, ts):
    def tile_pair(it, st):
        st = _hgrn_core_tile(2 * it, st, q_ref, *refs, ts=ts)
        return _hgrn_core_tile(2 * it + 1, st, q_ref, *refs, ts=ts)

    lax.fori_loop(0, q_ref.shape[1] // (2 * ts), tile_pair,
                  jnp.zeros((HGRN_HEAD_DIM, HGRN_HEAD_DIM), F32))


def _hgrn_core_tile(s, st, q_ref, lf_ref, k_ref, i_ref, sg_ref, og_ref, o_ref, *, ts):
    C = HGRN_CHUNK
    r0 = pl.multiple_of(s * ts, ts)
    rows = pl.ds(r0, ts)
    b_all = _chunk_cumsum(lf_ref[0, rows, :])
    q_all = q_ref[0, rows, :].astype(F32)
    k_all = k_ref[0, rows, :].astype(F32)
    q_dec_all = (q_all * jnp.exp(b_all)).astype(BF16)
    k_inv_all = (k_all * jnp.exp(-b_all)).astype(BF16)
    iv_all = i_ref[0, rows, :]
    crow = lax.broadcasted_iota(jnp.int32, (C, C), 0)
    ccol = lax.broadcasted_iota(jnp.int32, (C, C), 1)

    n_chunks = ts // C
    sls = [slice(n * C, (n + 1) * C) for n in range(n_chunks)]
    b_lasts = [b_all[(n + 1) * C - 1:(n + 1) * C, :] for n in range(n_chunks)]
    a_list, ut_list = [], []
    for n, sl in enumerate(sls):
        a_list.append(lax.dot_general(q_dec_all[sl, :], k_inv_all[sl, :], (((1,), (1,)), ((), ())),
                                      preferred_element_type=F32))
        k_state = (k_all[sl, :] * jnp.exp(b_lasts[n] - b_all[sl, :])).astype(BF16)
        ut_list.append(lax.dot_general(iv_all[sl, :], k_state, (((0,), (0,)), ((), ())),
                                       preferred_element_type=F32))
    st_list = []
    for n in range(n_chunks):
        st_list.append(st.astype(BF16))
        st = jnp.exp(b_lasts[n]) * st + ut_list[n]
    outs = []
    for n, sl in enumerate(sls):
        a = jnp.where(ccol <= crow, a_list[n], 0.0).astype(BF16)
        o = jnp.dot(a, iv_all[sl, :], preferred_element_type=F32)
        outs.append(o + lax.dot_general(q_dec_all[sl, :], st_list[n], (((1,), (1,)), ((), ())),
                                        preferred_element_type=F32))
    o = jnp.concatenate(outs, axis=0)
    o = o * lax.rsqrt(jnp.mean(o * o, axis=-1, keepdims=True) + RMS_EPS)
    o = o * og_ref[...] * sg_ref[0, rows, :].astype(F32)
    o_ref[0, rows, :] = o.astype(BF16)
    return st


def _hgrn_core(q, lf, k, iv, sg, og, *, ts):
    B, S, D = q.shape
    H = D // HGRN_HEAD_DIM
    seq = pl.BlockSpec((1, S, HGRN_HEAD_DIM), lambda b, h: (b, 0, h))
    return pl.pallas_call(
        functools.partial(_hgrn_core_kernel, ts=ts),
        grid=(B, H),
        in_specs=[seq] * 5 + [pl.BlockSpec((1, HGRN_HEAD_DIM), lambda b, h: (0, h))],
        out_specs=seq,
        out_shape=jax.ShapeDtypeStruct((B, S, D), BF16),
        compiler_params=pltpu.CompilerParams(
            dimension_semantics=("arbitrary", "arbitrary"), vmem_limit_bytes=VMEM_LIMIT),
        name="hgrn_core",
    )(q, lf, k, iv, sg, og)


def kernel(x, att_norm_g, att_w_in, att_b_f, att_w_out, hgrn_norm_g, hgrn_w_in, hgrn_lb_logits,
           hgrn_onorm_g, hgrn_w_out, ffn_norm_g, ffn_w_up, ffn_conv_w, ffn_conv_b, ffn_w_down,
           final_norm_g):
    B, S, D = x.shape
    depth = ffn_norm_g.shape[0]
    assert D == FOX_HEADS * FOX_HEAD_DIM and FOX_HEADS * AUG_LANES == LANES
    assert S % ROW_TILE == 0 and S % ATTN_TILE == 0
    assert S % (2 * HGRN_TILE) == 0 and HGRN_TILE % HGRN_CHUNK == 0
    assert (ffn_w_up.shape[2] // 2) % FFN_CHUNK == 0

    sm = jax.nn.softmax(hgrn_lb_logits.astype(F32), axis=0)
    lower_bounds = jnp.cumsum(sm, axis=0) - sm[0:1]
    fg = final_norm_g.reshape(1, D)
    hsum = (jnp.arange(D)[:, None] // FOX_HEAD_DIM
            == jnp.arange(LANES)[None, :] // AUG_LANES).astype(BF16)

    wup, taps, wd = _ffn_weights(ffn_w_up, ffn_conv_w, ffn_conv_b, ffn_w_down)

    for layer in range(depth):
        j = layer // 2
        if layer % 2 == 0:
            order = jnp.argsort(att_b_f[j])
            cols = (order[:, None] * FOX_HEAD_DIM + jnp.arange(FOX_HEAD_DIM)[None, :]).reshape(-1)
            w_in = att_w_in[j]
            wqkv = jnp.take(w_in[:, :3 * D].reshape(D, 3, FOX_HEADS, FOX_HEAD_DIM), order,
                            axis=2).reshape(D, 3 * D).astype(BF16)
            wf = jnp.repeat(jnp.take(w_in[:, 3 * D:], order, axis=1), AUG_LANES,
                            axis=1).astype(BF16)
            bf = jnp.repeat(jnp.take(att_b_f[j], order), AUG_LANES).reshape(1, LANES)
            qkv, augq, augk, stats = _fox_in(x, att_norm_g[j].reshape(1, D), wqkv, wf, bf, hsum,
                                             ts=ATTN_TILE)
            o = _fox_attn(_attn_block_counts(stats), qkv, augq, augk, t=ATTN_TILE)
            w_out = jnp.take(att_w_out[j], cols, axis=0)
        else:
            q, lf, k, iv, sg = _hgrn_in(x, hgrn_norm_g[j].reshape(1, D), hgrn_w_in[j].astype(BF16),
                                        lower_bounds[layer].reshape(1, D), ts=HGRN_TILE)
            o = _hgrn_core(q, lf, k, iv, sg, hgrn_onorm_g[j].reshape(1, D), ts=HGRN_TILE)
            w_out = hgrn_w_out[j]
        x = _ffn(x, o, w_out.astype(BF16), ffn_norm_g[layer].reshape(1, D), wup, taps, wd, fg,
                 layer=layer, ts=ROW_TILE, final=(layer == depth - 1))
    return x
```

```python
import functools
import math

import jax
import jax.numpy as jnp
from jax import lax
from jax.experimental import pallas as pl
from jax.experimental.pallas import tpu as pltpu

F32 = jnp.float32
BF16 = jnp.bfloat16

RMS_EPS = 1e-6
FOX_HEADS = 16
FOX_HEAD_DIM = 64
HGRN_HEAD_DIM = 128
HGRN_CHUNK = 64
CONV_WIDTH = 3
LOG2E = math.log2(math.e)

LANES = 128
AUG_LANES = 8
NEG_BIG = -1e30
NORM_SLACK = 1.01
EXP2_ZERO_GAP = 152.0
VMEM_LIMIT = 56 * 1024 * 1024

ROW_TILE = 512
ATTN_TILE = 512
ATTN_STATIC_BLOCKS = 5
HGRN_TILE = 1024
FFN_CHUNK = 256
FFN_UP_LEAD = 4
FFN_STAGES = FFN_UP_LEAD + 1


def _rmsnorm(x, g):
    ms = jnp.mean(x * x, axis=-1, keepdims=True)
    return x * lax.rsqrt(ms + RMS_EPS) * g


def _split3(x):
    hi = x.astype(BF16)
    r = x - hi.astype(F32)
    mid = r.astype(BF16)
    lo = (r - mid.astype(F32)).astype(BF16)
    return hi, mid, lo


def _silu(x):
    return x * (1.0 / (1.0 + jnp.exp(-x)))


def _tile_lanes(x, n):
    return jnp.concatenate([x] * n, axis=1) if n > 1 else x


def _fox_in_kernel(x_ref, g_ref, wqkv_ref, wf_ref, bf_ref, hsum_ref, qkv_ref, augq_ref, augk_ref,
                   stats_ref, carry_ref, *, d_model, q_scale):
    ts = x_ref.shape[1]

    @pl.when(pl.program_id(1) == 0)
    def _():
        carry_ref[...] = jnp.zeros_like(carry_ref)

    h = _rmsnorm(x_ref[0], g_ref[...]).astype(BF16)

    z = jnp.dot(h, wf_ref[...], preferred_element_type=F32) + bf_ref[...]
    logf = jnp.minimum(z, 0.0) - jnp.log1p(jnp.exp(-jnp.abs(z)))
    lc = logf * LOG2E
    row = lax.broadcasted_iota(jnp.int32, (ts, ts), 0)
    col = lax.broadcasted_iota(jnp.int32, (ts, ts), 1)
    tri = jnp.where(col <= row, 1.0, 0.0).astype(BF16)
    hi, mid, lo = _split3(lc)
    cs = (jnp.dot(tri, hi, preferred_element_type=F32)
          + jnp.dot(tri, mid, preferred_element_type=F32)
          + jnp.dot(tri, lo, preferred_element_type=F32)) + carry_ref[...]
    carry_ref[...] = cs[ts - 1:ts, :]

    c_hi, c_mid, c_lo = (t.astype(F32) for t in _split3(cs))
    j = lax.broadcasted_iota(jnp.int32, (ts, LANES), 1) & (AUG_LANES - 1)
    augq = jnp.where(j == 0, c_hi, jnp.where(j == 1, c_mid, jnp.where(j == 2, c_lo,
                     jnp.where(j < 6, 1.0, 0.0))))
    augk = jnp.where(j < 3, 1.0, jnp.where(j == 3, -c_hi, jnp.where(j == 4, -c_mid,
                     jnp.where(j == 5, -c_lo, 0.0))))
    augq_ref[0] = augq.astype(BF16)
    augk_ref[0] = augk.astype(BF16)

    def project(c):
        acc = jnp.dot(h, wqkv_ref[:, c * d_model:(c + 1) * d_model], preferred_element_type=F32)
        if c == 0:
            acc = acc * q_scale
        yb = acc.astype(BF16)
        qkv_ref[0, :, c * d_model:(c + 1) * d_model] = yb
        return yb.astype(F32)

    stored = [project(0), project(1)]
    norm2 = [jnp.dot(jnp.square(y).astype(BF16), hsum_ref[...], preferred_element_type=F32)
             * NORM_SLACK for y in stored]
    diag = jnp.dot((stored[0] * stored[1]).astype(BF16), hsum_ref[...], preferred_element_type=F32)
    diag_lb = diag - jnp.sqrt(norm2[0] * norm2[1]) * (2.0 ** -8)
    project(2)

    stats_ref[0, 0] = jnp.concatenate(
        [jnp.max(norm2[0], axis=0, keepdims=True), jnp.max(norm2[1], axis=0, keepdims=True),
         jnp.max(cs - diag_lb, axis=0, keepdims=True), cs[ts - 1:ts, :],
         jnp.zeros((4, LANES), F32)], axis=0)


def _fox_in(x, g, wqkv, wf, bf, hsum, *, ts):
    B, S, D = x.shape
    q_scale = LOG2E / math.sqrt(FOX_HEAD_DIM)
    const = lambda b, s: (0, 0)
    return pl.pallas_call(
        functools.partial(_fox_in_kernel, d_model=D, q_scale=q_scale),
        grid=(B, S // ts),
        in_specs=[
            pl.BlockSpec((1, ts, D), lambda b, s: (b, s, 0)),
            pl.BlockSpec((1, D), const),
            pl.BlockSpec((D, 3 * D), const),
            pl.BlockSpec((D, LANES), const),
            pl.BlockSpec((1, LANES), const),
            pl.BlockSpec((D, LANES), const),
        ],
        out_specs=[
            pl.BlockSpec((1, ts, 3 * D), lambda b, s: (b, s, 0)),
            pl.BlockSpec((1, ts, LANES), lambda b, s: (b, s, 0)),
            pl.BlockSpec((1, ts, LANES), lambda b, s: (b, s, 0)),
            pl.BlockSpec((1, 1, 8, LANES), lambda b, s: (b, s, 0, 0)),
        ],
        out_shape=[
            jax.ShapeDtypeStruct((B, S, 3 * D), BF16),
            jax.ShapeDtypeStruct((B, S, LANES), BF16),
            jax.ShapeDtypeStruct((B, S, LANES), BF16),
            jax.ShapeDtypeStruct((B, S // ts, 8, LANES), F32),
        ],
        scratch_shapes=[pltpu.VMEM((1, LANES), F32)],
        compiler_params=pltpu.CompilerParams(
            dimension_semantics=("arbitrary", "arbitrary"), vmem_limit_bytes=VMEM_LIMIT),
        name="fox_in",
    )(x, g, wqkv, wf, bf, hsum)


def _attn_block_counts(stats):
    B, nt = stats.shape[0], stats.shape[1]
    per_head = stats[:, :, :4, ::AUG_LANES]
    qn2, kn2, c_minus_diag, c_last = (per_head[:, :, r] for r in range(4))
    kn2_max = jnp.max(kn2, axis=1, keepdims=True)
    bound = jnp.sqrt(qn2 * kn2_max) + c_minus_diag + 1.0
    gap = bound[:, :, None, :] - c_last[:, None, :, :]
    kk = lax.broadcasted_iota(jnp.int32, (nt, nt), 1)
    ii = lax.broadcasted_iota(jnp.int32, (nt, nt), 0)
    need = (gap > -EXP2_ZERO_GAP) & (kk < ii)[None, :, :, None]
    oldest = jnp.min(jnp.where(need, kk[None, :, :, None], nt), axis=2)
    count = jnp.maximum(jnp.arange(nt)[None, :, None] - oldest, 0) + 1
    pairs = jnp.max(count.reshape(B, nt, -1, 2), axis=-1)
    return jnp.transpose(pairs, (0, 2, 1)).reshape(-1).astype(jnp.int32)


def _fox_attn_kernel(nblk_ref, q_ref, *refs, t):
    j = pl.program_id(1)
    nq = q_ref.shape[1] // t

    def tile(i, carry):
        n = nblk_ref[(pl.program_id(0) * pl.num_programs(1) + j) * nq + i]
        _fox_attn_tile(i, n, j, q_ref, *refs, t=t)
        return carry

    lax.fori_loop(0, nq, tile, 0)


def _fox_attn_tile(i, n, j, q_ref, augq_ref, k_ref, v_ref, augk_ref, o_ref, lhs_ref, m_ref,
                   acc_ref, sa_ref, sb_ref, pa_ref, pb_ref, ala_ref, alb_ref, mxa_ref, mxb_ref,
                   *, t):
    lane = lax.broadcasted_iota(jnp.int32, (t, LANES), 1)
    q_rows = pl.ds(pl.multiple_of(i * t, t), t)
    q2 = q_ref[0, q_rows, :]
    aq = augq_ref[0, q_rows, :]
    zero = jnp.zeros((), BF16)
    for hh in range(2):
        lo = (2 * j + hh) * AUG_LANES
        qm = jnp.where((lane >= FOX_HEAD_DIM * hh) & (lane < FOX_HEAD_DIM * (hh + 1)), q2, zero)
        am = jnp.where((lane >= lo) & (lane < lo + AUG_LANES), aq, zero)
        lhs_ref[hh] = jnp.concatenate([qm, am], axis=1)
        m_ref[hh] = jnp.full((t, LANES), NEG_BIG, F32)
        acc_ref[hh] = jnp.zeros((t, LANES), F32)

    one = jnp.ones((), BF16)
    row = lax.broadcasted_iota(jnp.int32, (t, t), 0)
    col = lax.broadcasted_iota(jnp.int32, (t, t), 1)

    def scores(jb, buf, masked=False):
        s_ref, mx_ref, _, _ = buf
        off = pl.multiple_of((i - jb) * t, t)
        rhs = jnp.concatenate([k_ref[0, pl.ds(off, t), :], augk_ref[0, pl.ds(off, t), :]], axis=1)
        for hh in range(2):
            s = lax.dot_general(lhs_ref[hh], rhs, (((1,), (1,)), ((), ())),
                                preferred_element_type=F32)
            if masked:
                s = jnp.where(col <= row, s, NEG_BIG)
            s_ref[hh] = s
            mx = s[:, :LANES]
            for c in range(1, t // LANES):
                mx = jnp.maximum(mx, s[:, c * LANES:(c + 1) * LANES])
            mx_ref[hh] = mx

    def softmax(buf):
        s_ref, mx_ref, p_ref, al_ref = buf
        for hh in range(2):
            m_prev = m_ref[hh]
            m_new = jnp.maximum(m_prev, jnp.max(mx_ref[hh], axis=1, keepdims=True))
            al_ref[hh] = jnp.exp2(m_prev - m_new)
            p_ref[hh] = jnp.exp2(s_ref[hh] - _tile_lanes(m_new, t // LANES)).astype(BF16)
            m_ref[hh] = m_new

    def values(jb, buf):
        _, _, p_ref, al_ref = buf
        off = pl.multiple_of((i - jb) * t, t)
        v2 = v_ref[0, pl.ds(off, t), :]
        for hh in range(2):
            own = (lane >= FOX_HEAD_DIM * hh) & (lane < FOX_HEAD_DIM * (hh + 1))
            pv = jnp.dot(p_ref[hh], jnp.where(own, v2, one), preferred_element_type=F32)
            acc_ref[hh] = al_ref[hh] * acc_ref[hh] + pv

    buf_a = (sa_ref, mxa_ref, pa_ref, ala_ref)
    buf_b = (sb_ref, mxb_ref, pb_ref, alb_ref)

    def run_static(n0):
        bufs = (buf_a, buf_b)
        for tau in range(n0 + 2):
            if 0 <= tau - 2 < n0:
                values(tau - 2, bufs[tau % 2])
            if 0 <= tau - 1 < n0:
                softmax(bufs[(tau - 1) % 2])
            if tau < n0:
                scores(tau, bufs[tau % 2], masked=(tau == 0))

    for n0 in range(1, ATTN_STATIC_BLOCKS + 1):
        pl.when(n == n0)(functools.partial(run_static, n0))

    def step_pair(u, carry):
        tau = 2 + 2 * u
        values(tau - 2, buf_a)
        softmax(buf_b)
        scores(tau, buf_a)
        values(tau - 1, buf_b)
        softmax(buf_a)
        scores(tau + 1, buf_b)
        return carry

    def drain_even():
        values(n - 2, buf_a)
        softmax(buf_b)
        values(n - 1, buf_b)

    def drain_odd():
        values(n - 3, buf_a)
        softmax(buf_b)
        scores(n - 1, buf_a)
        values(n - 2, buf_b)
        softmax(buf_a)
        values(n - 1, buf_a)

    @pl.when(n > ATTN_STATIC_BLOCKS)
    def _():
        scores(0, buf_a, masked=True)
        softmax(buf_a)
        scores(1, buf_b)
        pairs = (n - 2) // 2
        lax.fori_loop(0, pairs - 1, step_pair, 0)
        for parity, drain in ((0, drain_even), (1, drain_odd)):
            @pl.when(n % 2 == parity)
            def _():
                step_pair(pairs - 1, 0)
                drain()

    out_a = acc_ref[0] / pltpu.roll(acc_ref[0], FOX_HEAD_DIM, axis=1)
    out_b = acc_ref[1] / pltpu.roll(acc_ref[1], FOX_HEAD_DIM, axis=1)
    o_ref[0, q_rows, :] = jnp.where(lane < FOX_HEAD_DIM, out_a, out_b).astype(BF16)


def _fox_attn(nblk, qkv, augq, augk, *, t):
    B, S, D3 = qkv.shape
    D = D3 // 3
    nb = D // LANES
    grid_spec = pltpu.PrefetchScalarGridSpec(
        num_scalar_prefetch=1,
        grid=(B, nb),
        in_specs=[
            pl.BlockSpec((1, S, LANES), lambda b, j, nblk: (b, 0, j)),
            pl.BlockSpec((1, S, LANES), lambda b, j, nblk: (b, 0, 0)),
            pl.BlockSpec((1, S, LANES), lambda b, j, nblk: (b, 0, nb + j)),
            pl.BlockSpec((1, S, LANES), lambda b, j, nblk: (b, 0, 2 * nb + j)),
            pl.BlockSpec((1, S, LANES), lambda b, j, nblk: (b, 0, 0)),
        ],
        out_specs=pl.BlockSpec((1, S, LANES), lambda b, j, nblk: (b, 0, j)),
        scratch_shapes=[
            pltpu.VMEM((2, t, 2 * LANES), BF16),
            pltpu.VMEM((2, t, LANES), F32),
            pltpu.VMEM((2, t, LANES), F32),
            pltpu.VMEM((2, t, t), F32),
            pltpu.VMEM((2, t, t), F32),
            pltpu.VMEM((2, t, t), BF16),
            pltpu.VMEM((2, t, t), BF16),
            pltpu.VMEM((2, t, LANES), F32),
            pltpu.VMEM((2, t, LANES), F32),
            pltpu.VMEM((2, t, LANES), F32),
            pltpu.VMEM((2, t, LANES), F32),
        ],
    )
    return pl.pallas_call(
        functools.partial(_fox_attn_kernel, t=t),
        grid_spec=grid_spec,
        out_shape=jax.ShapeDtypeStruct((B, S, D), BF16),
        compiler_params=pltpu.CompilerParams(
            dimension_semantics=("arbitrary", "arbitrary"), vmem_limit_bytes=VMEM_LIMIT),
        name="fox_attn",
    )(nblk, qkv, augq, qkv, qkv, augk)


def _ffn_kernel(x_ref, o_ref, wout_ref, g_ref, wup_ref, taps_ref, wd_ref, fg_ref,
                out_ref, tail_ref, x1_ref, h_ref, acc_ref, u_ref, z_ref, *, final):
    ts = x_ref.shape[1]
    ffn_dim = wd_ref.shape[0]
    n_chunks = ffn_dim // FFN_CHUNK

    def cols(c, stream=0):
        lo = stream * ffn_dim + c * FFN_CHUNK
        return slice(lo, lo + FFN_CHUNK)

    @pl.when(pl.program_id(1) == 0)
    def _():
        tail_ref[...] = jnp.zeros_like(tail_ref)

    x1 = x_ref[0] + jnp.dot(o_ref[0], wout_ref[...], preferred_element_type=F32)
    x1_ref[...] = x1
    h_ref[...] = _rmsnorm(x1, g_ref[...]).astype(BF16)
    acc_ref[...] = jnp.zeros_like(acc_ref)
    def up(c):
        h = h_ref[...]
        for stream in range(2):
            u = jnp.dot(h, wup_ref[:, cols(c, stream)], preferred_element_type=F32)
            u_ref[c % FFN_STAGES, stream, 0:8, :] = tail_ref[stream, c]
            u_ref[c % FFN_STAGES, stream, 8:, :] = u
            tail_ref[stream, c] = u[ts - 8:, :]

    def conv(c, stream, cw):
        u0 = u_ref[c % FFN_STAGES, stream, 8:8 + ts, :]
        u1 = u_ref[c % FFN_STAGES, stream, 7:7 + ts, :]
        u2 = u_ref[c % FFN_STAGES, stream, 6:6 + ts, :]
        return cw[3:4, :] + cw[0:1, :] * u2 + cw[1:2, :] * u1 + cw[2:3, :] * u0

    def act(c):
        gate = conv(c, 0, taps_ref[:, cols(c, 0)])
        val = conv(c, 1, taps_ref[:, cols(c, 1)])
        z_ref[c % FFN_STAGES] = (_silu(gate) * val).astype(BF16)

    def down(c):
        acc_ref[...] += jnp.dot(z_ref[c % FFN_STAGES], wd_ref[cols(c), :],
                                preferred_element_type=F32)

    for c in range(min(FFN_UP_LEAD, n_chunks)):
        up(c)
    for c in range(min(FFN_UP_LEAD - 1, n_chunks)):
        act(c)
    for c in range(n_chunks):
        if c + FFN_UP_LEAD < n_chunks:
            up(c + FFN_UP_LEAD)
        if c + FFN_UP_LEAD - 1 < n_chunks:
            act(c + FFN_UP_LEAD - 1)
        down(c)
    y = x1_ref[...] + acc_ref[...]
    if final:
        y = _rmsnorm(y, fg_ref[...])
    out_ref[0] = y


def _ffn(x, o, wout, g, wup, taps, wd, fg, *, layer, ts, final):
    B, S, D = x.shape
    n_chunks = wd.shape[1] // FFN_CHUNK
    c2 = lambda b, s: (0, 0)
    of_layer = lambda b, s: (layer, 0, 0)
    return pl.pallas_call(
        functools.partial(_ffn_kernel, final=final),
        grid=(B, S // ts),
        in_specs=[
            pl.BlockSpec((1, ts, D), lambda b, s: (b, s, 0)),
            pl.BlockSpec((1, ts, D), lambda b, s: (b, s, 0)),
            pl.BlockSpec((D, D), c2),
            pl.BlockSpec((1, D), c2),
            pl.BlockSpec((None,) + wup.shape[1:], of_layer),
            pl.BlockSpec((None,) + taps.shape[1:], of_layer),
            pl.BlockSpec((None,) + wd.shape[1:], of_layer),
            pl.BlockSpec((1, D), c2),
        ],
        out_specs=pl.BlockSpec((1, ts, D), lambda b, s: (b, s, 0)),
        out_shape=jax.ShapeDtypeStruct((B, S, D), F32),
        scratch_shapes=[
            pltpu.VMEM((2, n_chunks, 8, FFN_CHUNK), F32),
            pltpu.VMEM((ts, D), F32),
            pltpu.VMEM((ts, D), BF16),
            pltpu.VMEM((ts, D), F32),
            pltpu.VMEM((FFN_STAGES, 2, 8 + ts, FFN_CHUNK), F32),
            pltpu.VMEM((FFN_STAGES, ts, FFN_CHUNK), BF16),
        ],
        compiler_params=pltpu.CompilerParams(
            dimension_semantics=("arbitrary", "arbitrary"), vmem_limit_bytes=VMEM_LIMIT),
        name="ffn_final" if final else "ffn",
    )(x, o, wout, g, wup, taps, wd, fg)


def _ffn_weights(w_up, conv_w, conv_b, w_down):
    depth, _, f2 = conv_w.shape
    taps = jnp.concatenate(
        [conv_w, conv_b[:, None, :], jnp.zeros((depth, 8 - CONV_WIDTH - 1, f2), F32)], axis=1)
    return w_up.astype(BF16), taps, w_down.astype(BF16)


def _hgrn_in_kernel(x_ref, g_ref, w_ref, lb_ref, q_ref, lf_ref, k_ref, i_ref, sg_ref, *, d_model):
    h = _rmsnorm(x_ref[0], g_ref[...]).astype(BF16)
    D = d_model
    q = jnp.dot(h, w_ref[:, 0:D], preferred_element_type=F32)
    q_ref[0] = _silu(q).astype(BF16)
    fl = jnp.dot(h, w_ref[:, D:2 * D], preferred_element_type=F32)
    lb = lb_ref[...]
    f = lb + (1.0 - lb) * (1.0 / (1.0 + jnp.exp(-fl)))
    lf_ref[0] = jnp.log(f)
    k_ref[0] = (1.0 - f).astype(BF16)
    gg = jnp.dot(h, w_ref[:, 3 * D:4 * D], preferred_element_type=F32)
    sg_ref[0] = _silu(gg).astype(BF16)
    i_ref[0] = jnp.dot(h, w_ref[:, 2 * D:3 * D], preferred_element_type=F32).astype(BF16)


def _hgrn_in(x, g, w, lb, *, ts):
    B, S, D = x.shape
    const = lambda b, s: (0, 0)
    tile = pl.BlockSpec((1, ts, D), lambda b, s: (b, s, 0))
    return pl.pallas_call(
        functools.partial(_hgrn_in_kernel, d_model=D),
        grid=(B, S // ts),
        in_specs=[tile, pl.BlockSpec((1, D), const), pl.BlockSpec((D, 4 * D), const),
                  pl.BlockSpec((1, D), const)],
        out_specs=[tile] * 5,
        out_shape=[
            jax.ShapeDtypeStruct((B, S, D), BF16),
            jax.ShapeDtypeStruct((B, S, D), F32),
            jax.ShapeDtypeStruct((B, S, D), BF16),
            jax.ShapeDtypeStruct((B, S, D), BF16),
            jax.ShapeDtypeStruct((B, S, D), BF16),
        ],
        compiler_params=pltpu.CompilerParams(
            dimension_semantics=("arbitrary", "arbitrary"), vmem_limit_bytes=VMEM_LIMIT),
        name="hgrn_in",
    )(x, g, w, lb)


def _chunk_cumsum(x):
    r = lax.broadcasted_iota(jnp.int32, x.shape, 0) & (HGRN_CHUNK - 1)
    k = 1
    while k < HGRN_CHUNK:
        x = x + jnp.where(r >= k, pltpu.roll(x, k, axis=0), 0.0)
        k *= 2
    return x


def _hgrn_core_kernel(q_ref, *refs, ts):
    def tile_pair(it, st):
        st = _hgrn_core_tile(2 * it, st, q_ref, *refs, ts=ts)
        return _hgrn_core_tile(2 * it + 1, st, q_ref, *refs, ts=ts)

    lax.fori_loop(0, q_ref.shape[1] // (2 * ts), tile_pair,
                  jnp.zeros((HGRN_HEAD_DIM, HGRN_HEAD_DIM), F32))


def _hgrn_core_tile(s, st, q_ref, lf_ref, k_ref, i_ref, sg_ref, og_ref, o_ref, *, ts):
    C = HGRN_CHUNK
    r0 = pl.multiple_of(s * ts, ts)
    rows = pl.ds(r0, ts)
    b_all = _chunk_cumsum(lf_ref[0, rows, :])
    q_all = q_ref[0, rows, :].astype(F32)
    k_all = k_ref[0, rows, :].astype(F32)
    q_dec_all = (q_all * jnp.exp(b_all)).astype(BF16)
    k_inv_all = (k_all * jnp.exp(-b_all)).astype(BF16)
    iv_all = i_ref[0, rows, :]
    crow = lax.broadcasted_iota(jnp.int32, (C, C), 0)
    ccol = lax.broadcasted_iota(jnp.int32, (C, C), 1)

    n_chunks = ts // C
    sls = [slice(n * C, (n + 1) * C) for n in range(n_chunks)]
    b_lasts = [b_all[(n + 1) * C - 1:(n + 1) * C, :] for n in range(n_chunks)]
    a_list, ut_list = [], []
    for n, sl in enumerate(sls):
        a_list.append(lax.dot_general(q_dec_all[sl, :], k_inv_all[sl, :], (((1,), (1,)), ((), ())),
                                      preferred_element_type=F32))
        k_state = (k_all[sl, :] * jnp.exp(b_lasts[n] - b_all[sl, :])).astype(BF16)
        ut_list.append(lax.dot_general(iv_all[sl, :], k_state, (((0,), (0,)), ((), ())),
                                       preferred_element_type=F32))
    st_list = []
    for n in range(n_chunks):
        st_list.append(st.astype(BF16))
        st = jnp.exp(b_lasts[n]) * st + ut_list[n]
    outs = []
    for n, sl in enumerate(sls):
        a = jnp.where(ccol <= crow, a_list[n], 0.0).astype(BF16)
        o = jnp.dot(a, iv_all[sl, :], preferred_element_type=F32)
        outs.append(o + lax.dot_general(q_dec_all[sl, :], st_list[n], (((1,), (1,)), ((), ())),
                                        preferred_element_type=F32))
    o = jnp.concatenate(outs, axis=0)
    o = o * lax.rsqrt(jnp.mean(o * o, axis=-1, keepdims=True) + RMS_EPS)
    o = o * og_ref[...] * sg_ref[0, rows, :].astype(F32)
    o_ref[0, rows, :] = o.astype(BF16)
    return st


def _hgrn_core(q, lf, k, iv, sg, og, *, ts):
    B, S, D = q.shape
    H = D // HGRN_HEAD_DIM
    seq = pl.BlockSpec((1, S, HGRN_HEAD_DIM), lambda b, h: (b, 0, h))
    return pl.pallas_call(
        functools.partial(_hgrn_core_kernel, ts=ts),
        grid=(B, H),
        in_specs=[seq] * 5 + [pl.BlockSpec((1, HGRN_HEAD_DIM), lambda b, h: (0, h))],
        out_specs=seq,
        out_shape=jax.ShapeDtypeStruct((B, S, D), BF16),
        compiler_params=pltpu.CompilerParams(
            dimension_semantics=("arbitrary", "arbitrary"), vmem_limit_bytes=VMEM_LIMIT),
        name="hgrn_core",
    )(q, lf, k, iv, sg, og)


def kernel(x, att_norm_g, att_w_in, att_b_f, att_w_out, hgrn_norm_g, hgrn_w_in, hgrn_lb_logits,
           hgrn_onorm_g, hgrn_w_out, ffn_norm_g, ffn_w_up, ffn_conv_w, ffn_conv_b, ffn_w_down,
           final_norm_g):
    B, S, D = x.shape
    depth = ffn_norm_g.shape[0]
    assert D == FOX_HEADS * FOX_HEAD_DIM and FOX_HEADS * AUG_LANES == LANES
    assert S % ROW_TILE == 0 and S % ATTN_TILE == 0
    assert S % (2 * HGRN_TILE) == 0 and HGRN_TILE % HGRN_CHUNK == 0
    assert (ffn_w_up.shape[2] // 2) % FFN_CHUNK == 0

    sm = jax.nn.softmax(hgrn_lb_logits.astype(F32), axis=0)
    lower_bounds = jnp.cumsum(sm, axis=0) - sm[0:1]
    fg = final_norm_g.reshape(1, D)
    hsum = (jnp.arange(D)[:, None] // FOX_HEAD_DIM
            == jnp.arange(LANES)[None, :] // AUG_LANES).astype(BF16)

    wup, taps, wd = _ffn_weights(ffn_w_up, ffn_conv_w, ffn_conv_b, ffn_w_down)

    for layer in range(depth):
        j = layer // 2
        if layer % 2 == 0:
            order = jnp.argsort(att_b_f[j])
            cols = (order[:, None] * FOX_HEAD_DIM + jnp.arange(FOX_HEAD_DIM)[None, :]).reshape(-1)
            w_in = att_w_in[j]
            wqkv = jnp.take(w_in[:, :3 * D].reshape(D, 3, FOX_HEADS, FOX_HEAD_DIM), order,
                            axis=2).reshape(D, 3 * D).astype(BF16)
            wf = jnp.repeat(jnp.take(w_in[:, 3 * D:], order, axis=1), AUG_LANES,
                            axis=1).astype(BF16)
            bf = jnp.repeat(jnp.take(att_b_f[j], order), AUG_LANES).reshape(1, LANES)
            qkv, augq, augk, stats = _fox_in(x, att_norm_g[j].reshape(1, D), wqkv, wf, bf, hsum,
                                             ts=ATTN_TILE)
            o = _fox_attn(_attn_block_counts(stats), qkv, augq, augk, t=ATTN_TILE)
            w_out = jnp.take(att_w_out[j], cols, axis=0)
        else:
            q, lf, k, iv, sg = _hgrn_in(x, hgrn_norm_g[j].reshape(1, D), hgrn_w_in[j].astype(BF16),
                                        lower_bounds[layer].reshape(1, D), ts=HGRN_TILE)
            o = _hgrn_core(q, lf, k, iv, sg, hgrn_onorm_g[j].reshape(1, D), ts=HGRN_TILE)
            w_out = hgrn_w_out[j]
        x = _ffn(x, o, w_out.astype(BF16), ffn_norm_g[layer].reshape(1, D), wup, taps, wd, fg,
                 layer=layer, ts=ROW_TILE, final=(layer == depth - 1))
    return x
```

```python
import functools
import math

import jax
import jax.numpy as jnp
from jax import lax
from jax.experimental import pallas as pl
from jax.experimental.pallas import tpu as pltpu

F32 = jnp.float32
BF16 = jnp.bfloat16

RMS_EPS = 1e-6
FOX_HEADS = 16
FOX_HEAD_DIM = 64
HGRN_HEAD_DIM = 128
HGRN_CHUNK = 64
CONV_WIDTH = 3
LOG2E = math.log2(math.e)

LANES = 128
AUG_LANES = 8
NEG_BIG = -1e30
NORM_SLACK = 1.01
EXP2_ZERO_GAP = 152.0
VMEM_LIMIT = 56 * 1024 * 1024

ROW_TILE = 512
ATTN_TILE = 512
ATTN_STATIC_BLOCKS = 8
HGRN_TILE = 1024
FFN_CHUNK = 256
FFN_UP_LEAD = 4
FFN_STAGES = FFN_UP_LEAD + 1


def _rmsnorm(x, g):
    ms = jnp.mean(x * x, axis=-1, keepdims=True)
    return x * lax.rsqrt(ms + RMS_EPS) * g


def _split3(x):
    hi = x.astype(BF16)
    r = x - hi.astype(F32)
    mid = r.astype(BF16)
    lo = (r - mid.astype(F32)).astype(BF16)
    return hi, mid, lo


def _silu(x):
    return x * (1.0 / (1.0 + jnp.exp(-x)))


def _tile_lanes(x, n):
    return jnp.concatenate([x] * n, axis=1) if n > 1 else x


def _fox_in_kernel(x_ref, g_ref, wqkv_ref, wf_ref, bf_ref, hsum_ref, qkv_ref, augq_ref, augk_ref,
                   stats_ref, carry_ref, *, d_model, q_scale):
    ts = x_ref.shape[1]

    @pl.when(pl.program_id(1) == 0)
    def _():
        carry_ref[...] = jnp.zeros_like(carry_ref)

    h = _rmsnorm(x_ref[0], g_ref[...]).astype(BF16)

    z = jnp.dot(h, wf_ref[...], preferred_element_type=F32) + bf_ref[...]
    logf = jnp.minimum(z, 0.0) - jnp.log1p(jnp.exp(-jnp.abs(z)))
    lc = logf * LOG2E
    row = lax.broadcasted_iota(jnp.int32, (ts, ts), 0)
    col = lax.broadcasted_iota(jnp.int32, (ts, ts), 1)
    tri = jnp.where(col <= row, 1.0, 0.0).astype(BF16)
    hi, mid, lo = _split3(lc)
    cs = (jnp.dot(tri, hi, preferred_element_type=F32)
          + jnp.dot(tri, mid, preferred_element_type=F32)
          + jnp.dot(tri, lo, preferred_element_type=F32)) + carry_ref[...]
    carry_ref[...] = cs[ts - 1:ts, :]

    c_hi, c_mid, c_lo = (t.astype(F32) for t in _split3(cs))
    j = lax.broadcasted_iota(jnp.int32, (ts, LANES), 1) & (AUG_LANES - 1)
    augq = jnp.where(j == 0, c_hi, jnp.where(j == 1, c_mid, jnp.where(j == 2, c_lo,
                     jnp.where(j < 6, 1.0, 0.0))))
    augk = jnp.where(j < 3, 1.0, jnp.where(j == 3, -c_hi, jnp.where(j == 4, -c_mid,
                     jnp.where(j == 5, -c_lo, 0.0))))
    augq_ref[0] = augq.astype(BF16)
    augk_ref[0] = augk.astype(BF16)

    def project(c):
        acc = jnp.dot(h, wqkv_ref[:, c * d_model:(c + 1) * d_model], preferred_element_type=F32)
        if c == 0:
            acc = acc * q_scale
        yb = acc.astype(BF16)
        qkv_ref[0, :, c * d_model:(c + 1) * d_model] = yb
        return yb.astype(F32)

    stored = [project(0), project(1)]
    norm2 = [jnp.dot(jnp.square(y).astype(BF16), hsum_ref[...], preferred_element_type=F32)
             * NORM_SLACK for y in stored]
    diag = jnp.dot((stored[0] * stored[1]).astype(BF16), hsum_ref[...], preferred_element_type=F32)
    diag_lb = diag - jnp.sqrt(norm2[0] * norm2[1]) * (2.0 ** -8)
    project(2)

    stats_ref[0, 0] = jnp.concatenate(
        [jnp.max(norm2[0], axis=0, keepdims=True), jnp.max(norm2[1], axis=0, keepdims=True),
         jnp.max(cs - diag_lb, axis=0, keepdims=True), cs[ts - 1:ts, :],
         jnp.zeros((4, LANES), F32)], axis=0)


def _fox_in(x, g, wqkv, wf, bf, hsum, *, ts):
    B, S, D = x.shape
    q_scale = LOG2E / math.sqrt(FOX_HEAD_DIM)
    const = lambda b, s: (0, 0)
    return pl.pallas_call(
        functools.partial(_fox_in_kernel, d_model=D, q_scale=q_scale),
        grid=(B, S // ts),
        in_specs=[
            pl.BlockSpec((1, ts, D), lambda b, s: (b, s, 0)),
            pl.BlockSpec((1, D), const),
            pl.BlockSpec((D, 3 * D), const),
            pl.BlockSpec((D, LANES), const),
            pl.BlockSpec((1, LANES), const),
            pl.BlockSpec((D, LANES), const),
        ],
        out_specs=[
            pl.BlockSpec((1, ts, 3 * D), lambda b, s: (b, s, 0)),
            pl.BlockSpec((1, ts, LANES), lambda b, s: (b, s, 0)),
            pl.BlockSpec((1, ts, LANES), lambda b, s: (b, s, 0)),
            pl.BlockSpec((1, 1, 8, LANES), lambda b, s: (b, s, 0, 0)),
        ],
        out_shape=[
            jax.ShapeDtypeStruct((B, S, 3 * D), BF16),
            jax.ShapeDtypeStruct((B, S, LANES), BF16),
            jax.ShapeDtypeStruct((B, S, LANES), BF16),
            jax.ShapeDtypeStruct((B, S // ts, 8, LANES), F32),
        ],
        scratch_shapes=[pltpu.VMEM((1, LANES), F32)],
        compiler_params=pltpu.CompilerParams(
            dimension_semantics=("arbitrary", "arbitrary"), vmem_limit_bytes=VMEM_LIMIT),
        name="fox_in",
    )(x, g, wqkv, wf, bf, hsum)


def _attn_block_counts(stats):
    B, nt = stats.shape[0], stats.shape[1]
    per_head = stats[:, :, :4, ::AUG_LANES]
    qn2, kn2, c_minus_diag, c_last = (per_head[:, :, r] for r in range(4))
    kn2_max = jnp.max(kn2, axis=1, keepdims=True)
    bound = jnp.sqrt(qn2 * kn2_max) + c_minus_diag + 1.0
    gap = bound[:, :, None, :] - c_last[:, None, :, :]
    kk = lax.broadcasted_iota(jnp.int32, (nt, nt), 1)
    ii = lax.broadcasted_iota(jnp.int32, (nt, nt), 0)
    need = (gap > -EXP2_ZERO_GAP) & (kk < ii)[None, :, :, None]
    oldest = jnp.min(jnp.where(need, kk[None, :, :, None], nt), axis=2)
    count = jnp.maximum(jnp.arange(nt)[None, :, None] - oldest, 0) + 1
    pairs = jnp.max(count.reshape(B, nt, -1, 2), axis=-1)
    return jnp.transpose(pairs, (0, 2, 1)).reshape(-1).astype(jnp.int32)


def _fox_attn_kernel(nblk_ref, q_ref, *refs, t):
    j = pl.program_id(1)
    nq = q_ref.shape[1] // t

    def tile(i, carry):
        n = nblk_ref[(pl.program_id(0) * pl.num_programs(1) + j) * nq + i]
        _fox_attn_tile(i, n, j, q_ref, *refs, t=t)
        return carry

    lax.fori_loop(0, nq, tile, 0)


def _fox_attn_tile(i, n, j, q_ref, augq_ref, k_ref, v_ref, augk_ref, o_ref, lhs_ref, m_ref,
                   acc_ref, sa_ref, sb_ref, pa_ref, pb_ref, ala_ref, alb_ref, mxa_ref, mxb_ref,
                   *, t):
    lane = lax.broadcasted_iota(jnp.int32, (t, LANES), 1)
    q_rows = pl.ds(pl.multiple_of(i * t, t), t)
    q2 = q_ref[0, q_rows, :]
    aq = augq_ref[0, q_rows, :]
    zero = jnp.zeros((), BF16)
    for hh in range(2):
        lo = (2 * j + hh) * AUG_LANES
        qm = jnp.where((lane >= FOX_HEAD_DIM * hh) & (lane < FOX_HEAD_DIM * (hh + 1)), q2, zero)
        am = jnp.where((lane >= lo) & (lane < lo + AUG_LANES), aq, zero)
        lhs_ref[hh] = jnp.concatenate([qm, am], axis=1)
        m_ref[hh] = jnp.full((t, LANES), NEG_BIG, F32)
        acc_ref[hh] = jnp.zeros((t, LANES), F32)

    one = jnp.ones((), BF16)
    row = lax.broadcasted_iota(jnp.int32, (t, t), 0)
    col = lax.broadcasted_iota(jnp.int32, (t, t), 1)

    def scores(jb, buf, masked=False):
        s_ref, mx_ref, _, _ = buf
        off = pl.multiple_of((i - jb) * t, t)
        rhs = jnp.concatenate([k_ref[0, pl.ds(off, t), :], augk_ref[0, pl.ds(off, t), :]], axis=1)
        for hh in range(2):
            s = lax.dot_general(lhs_ref[hh], rhs, (((1,), (1,)), ((), ())),
                                preferred_element_type=F32)
            if masked:
                s = jnp.where(col <= row, s, NEG_BIG)
            s_ref[hh] = s
            mx = s[:, :LANES]
            for c in range(1, t // LANES):
                mx = jnp.maximum(mx, s[:, c * LANES:(c + 1) * LANES])
            mx_ref[hh] = mx

    def softmax(buf):
        s_ref, mx_ref, p_ref, al_ref = buf
        for hh in range(2):
            m_prev = m_ref[hh]
            m_new = jnp.maximum(m_prev, jnp.max(mx_ref[hh], axis=1, keepdims=True))
            al_ref[hh] = jnp.exp2(m_prev - m_new)
            p_ref[hh] = jnp.exp2(s_ref[hh] - _tile_lanes(m_new, t // LANES)).astype(BF16)
            m_ref[hh] = m_new

    def values(jb, buf):
        _, _, p_ref, al_ref = buf
        off = pl.multiple_of((i - jb) * t, t)
        v2 = v_ref[0, pl.ds(off, t), :]
        for hh in range(2):
            own = (lane >= FOX_HEAD_DIM * hh) & (lane < FOX_HEAD_DIM * (hh + 1))
            pv = jnp.dot(p_ref[hh], jnp.where(own, v2, one), preferred_element_type=F32)
            acc_ref[hh] = al_ref[hh] * acc_ref[hh] + pv

    buf_a = (sa_ref, mxa_ref, pa_ref, ala_ref)
    buf_b = (sb_ref, mxb_ref, pb_ref, alb_ref)

    def run_static(n0):
        bufs = (buf_a, buf_b)
        for tau in range(n0 + 2):
            if 0 <= tau - 2 < n0:
                values(tau - 2, bufs[tau % 2])
            if 0 <= tau - 1 < n0:
                softmax(bufs[(tau - 1) % 2])
            if tau < n0:
                scores(tau, bufs[tau % 2], masked=(tau == 0))

    for n0 in range(1, ATTN_STATIC_BLOCKS + 1):
        pl.when(n == n0)(functools.partial(run_static, n0))

    def step_pair(u, carry):
        tau = 2 + 2 * u
        values(tau - 2, buf_a)
        softmax(buf_b)
        scores(tau, buf_a)
        values(tau - 1, buf_b)
        softmax(buf_a)
        scores(tau + 1, buf_b)
        return carry

    def drain_even():
        values(n - 2, buf_a)
        softmax(buf_b)
        values(n - 1, buf_b)

    def drain_odd():
        values(n - 3, buf_a)
        softmax(buf_b)
        scores(n - 1, buf_a)
        values(n - 2, buf_b)
        softmax(buf_a)
        values(n - 1, buf_a)

    @pl.when(n > ATTN_STATIC_BLOCKS)
    def _():
        scores(0, buf_a, masked=True)
        softmax(buf_a)
        scores(1, buf_b)
        pairs = (n - 2) // 2
        lax.fori_loop(0, pairs - 1, step_pair, 0)
        for parity, drain in ((0, drain_even), (1, drain_odd)):
            @pl.when(n % 2 == parity)
            def _():
                step_pair(pairs - 1, 0)
                drain()

    out_a = acc_ref[0] / pltpu.roll(acc_ref[0], FOX_HEAD_DIM, axis=1)
    out_b = acc_ref[1] / pltpu.roll(acc_ref[1], FOX_HEAD_DIM, axis=1)
    o_ref[0, q_rows, :] = jnp.where(lane < FOX_HEAD_DIM, out_a, out_b).astype(BF16)


def _fox_attn(nblk, qkv, augq, augk, *, t):
    B, S, D3 = qkv.shape
    D = D3 // 3
    nb = D // LANES
    grid_spec = pltpu.PrefetchScalarGridSpec(
        num_scalar_prefetch=1,
        grid=(B, nb),
        in_specs=[
            pl.BlockSpec((1, S, LANES), lambda b, j, nblk: (b, 0, j)),
            pl.BlockSpec((1, S, LANES), lambda b, j, nblk: (b, 0, 0)),
            pl.BlockSpec((1, S, LANES), lambda b, j, nblk: (b, 0, nb + j)),
            pl.BlockSpec((1, S, LANES), lambda b, j, nblk: (b, 0, 2 * nb + j)),
            pl.BlockSpec((1, S, LANES), lambda b, j, nblk: (b, 0, 0)),
        ],
        out_specs=pl.BlockSpec((1, S, LANES), lambda b, j, nblk: (b, 0, j)),
        scratch_shapes=[
            pltpu.VMEM((2, t, 2 * LANES), BF16),
            pltpu.VMEM((2, t, LANES), F32),
            pltpu.VMEM((2, t, LANES), F32),
            pltpu.VMEM((2, t, t), F32),
            pltpu.VMEM((2, t, t), F32),
            pltpu.VMEM((2, t, t), BF16),
            pltpu.VMEM((2, t, t), BF16),
            pltpu.VMEM((2, t, LANES), F32),
            pltpu.VMEM((2, t, LANES), F32),
            pltpu.VMEM((2, t, LANES), F32),
            pltpu.VMEM((2, t, LANES), F32),
        ],
    )
    return pl.pallas_call(
        functools.partial(_fox_attn_kernel, t=t),
        grid_spec=grid_spec,
        out_shape=jax.ShapeDtypeStruct((B, S, D), BF16),
        compiler_params=pltpu.CompilerParams(
            dimension_semantics=("arbitrary", "arbitrary"), vmem_limit_bytes=VMEM_LIMIT),
        name="fox_attn",
    )(nblk, qkv, augq, qkv, qkv, augk)


def _ffn_kernel(x_ref, o_ref, wout_ref, g_ref, wup_ref, taps_ref, wd_ref, fg_ref,
                out_ref, tail_ref, x1_ref, h_ref, acc_ref, u_ref, z_ref, *, final):
    ts = x_ref.shape[1]
    ffn_dim = wd_ref.shape[0]
    n_chunks = ffn_dim // FFN_CHUNK

    def cols(c, stream=0):
        lo = stream * ffn_dim + c * FFN_CHUNK
        return slice(lo, lo + FFN_CHUNK)

    @pl.when(pl.program_id(1) == 0)
    def _():
        tail_ref[...] = jnp.zeros_like(tail_ref)

    x1 = x_ref[0] + jnp.dot(o_ref[0], wout_ref[...], preferred_element_type=F32)
    x1_ref[...] = x1
    h_ref[...] = _rmsnorm(x1, g_ref[...]).astype(BF16)
    acc_ref[...] = jnp.zeros_like(acc_ref)
    def up(c):
        h = h_ref[...]
        for stream in range(2):
            u = jnp.dot(h, wup_ref[:, cols(c, stream)], preferred_element_type=F32)
            u_ref[c % FFN_STAGES, stream, 0:8, :] = tail_ref[stream, c]
            u_ref[c % FFN_STAGES, stream, 8:, :] = u
            tail_ref[stream, c] = u[ts - 8:, :]

    def conv(c, stream, cw):
        u0 = u_ref[c % FFN_STAGES, stream, 8:8 + ts, :]
        u1 = u_ref[c % FFN_STAGES, stream, 7:7 + ts, :]
        u2 = u_ref[c % FFN_STAGES, stream, 6:6 + ts, :]
        return cw[3:4, :] + cw[0:1, :] * u2 + cw[1:2, :] * u1 + cw[2:3, :] * u0

    def act(c):
        gate = conv(c, 0, taps_ref[:, cols(c, 0)])
        val = conv(c, 1, taps_ref[:, cols(c, 1)])
        z_ref[c % FFN_STAGES] = (_silu(gate) * val).astype(BF16)

    def down(c):
        acc_ref[...] += jnp.dot(z_ref[c % FFN_STAGES], wd_ref[cols(c), :],
                                preferred_element_type=F32)

    for c in range(min(FFN_UP_LEAD, n_chunks)):
        up(c)
    for c in range(min(FFN_UP_LEAD - 1, n_chunks)):
        act(c)
    for c in range(n_chunks):
        if c + FFN_UP_LEAD < n_chunks:
            up(c + FFN_UP_LEAD)
        if c + FFN_UP_LEAD - 1 < n_chunks:
            act(c + FFN_UP_LEAD - 1)
        down(c)
    y = x1_ref[...] + acc_ref[...]
    if final:
        y = _rmsnorm(y, fg_ref[...])
    out_ref[0] = y


def _ffn(x, o, wout, g, wup, taps, wd, fg, *, layer, ts, final):
    B, S, D = x.shape
    n_chunks = wd.shape[1] // FFN_CHUNK
    c2 = lambda b, s: (0, 0)
    of_layer = lambda b, s: (layer, 0, 0)
    return pl.pallas_call(
        functools.partial(_ffn_kernel, final=final),
        grid=(B, S // ts),
        in_specs=[
            pl.BlockSpec((1, ts, D), lambda b, s: (b, s, 0)),
            pl.BlockSpec((1, ts, D), lambda b, s: (b, s, 0)),
            pl.BlockSpec((D, D), c2),
            pl.BlockSpec((1, D), c2),
            pl.BlockSpec((None,) + wup.shape[1:], of_layer),
            pl.BlockSpec((None,) + taps.shape[1:], of_layer),
            pl.BlockSpec((None,) + wd.shape[1:], of_layer),
            pl.BlockSpec((1, D), c2),
        ],
        out_specs=pl.BlockSpec((1, ts, D), lambda b, s: (b, s, 0)),
        out_shape=jax.ShapeDtypeStruct((B, S, D), F32),
        scratch_shapes=[
            pltpu.VMEM((2, n_chunks, 8, FFN_CHUNK), F32),
            pltpu.VMEM((ts, D), F32),
            pltpu.VMEM((ts, D), BF16),
            pltpu.VMEM((ts, D), F32),
            pltpu.VMEM((FFN_STAGES, 2, 8 + ts, FFN_CHUNK), F32),
            pltpu.VMEM((FFN_STAGES, ts, FFN_CHUNK), BF16),
        ],
        compiler_params=pltpu.CompilerParams(
            dimension_semantics=("arbitrary", "arbitrary"), vmem_limit_bytes=VMEM_LIMIT),
        name="ffn_final" if final else "ffn",
    )(x, o, wout, g, wup, taps, wd, fg)


def _ffn_weights(w_up, conv_w, conv_b, w_down):
    depth, _, f2 = conv_w.shape
    taps = jnp.concatenate(
        [conv_w, conv_b[:, None, :], jnp.zeros((depth, 8 - CONV_WIDTH - 1, f2), F32)], axis=1)
    return w_up.astype(BF16), taps, w_down.astype(BF16)


def _hgrn_in_kernel(x_ref, g_ref, w_ref, lb_ref, q_ref, lf_ref, k_ref, i_ref, sg_ref, *, d_model):
    h = _rmsnorm(x_ref[0], g_ref[...]).astype(BF16)
    D = d_model
    q = jnp.dot(h, w_ref[:, 0:D], preferred_element_type=F32)
    q_ref[0] = _silu(q).astype(BF16)
    fl = jnp.dot(h, w_ref[:, D:2 * D], preferred_element_type=F32)
    lb = lb_ref[...]
    f = lb + (1.0 - lb) * (1.0 / (1.0 + jnp.exp(-fl)))
    lf_ref[0] = jnp.log(f)
    k_ref[0] = (1.0 - f).astype(BF16)
    gg = jnp.dot(h, w_ref[:, 3 * D:4 * D], preferred_element_type=F32)
    sg_ref[0] = _silu(gg).astype(BF16)
    i_ref[0] = jnp.dot(h, w_ref[:, 2 * D:3 * D], preferred_element_type=F32).astype(BF16)


def _hgrn_in(x, g, w, lb, *, ts):
    B, S, D = x.shape
    const = lambda b, s: (0, 0)
    tile = pl.BlockSpec((1, ts, D), lambda b, s: (b, s, 0))
    return pl.pallas_call(
        functools.partial(_hgrn_in_kernel, d_model=D),
        grid=(B, S // ts),
        in_specs=[tile, pl.BlockSpec((1, D), const), pl.BlockSpec((D, 4 * D), const),
                  pl.BlockSpec((1, D), const)],
        out_specs=[tile] * 5,
        out_shape=[
            jax.ShapeDtypeStruct((B, S, D), BF16),
            jax.ShapeDtypeStruct((B, S, D), F32),
            jax.ShapeDtypeStruct((B, S, D), BF16),
            jax.ShapeDtypeStruct((B, S, D), BF16),
            jax.ShapeDtypeStruct((B, S, D), BF16),
        ],
        compiler_params=pltpu.CompilerParams(
            dimension_semantics=("arbitrary", "arbitrary"), vmem_limit_bytes=VMEM_LIMIT),
        name="hgrn_in",
    )(x, g, w, lb)


def _chunk_cumsum(x):
    r = lax.broadcasted_iota(jnp.int32, x.shape, 0) & (HGRN_CHUNK - 1)
    k = 1
    while k < HGRN_CHUNK:
        x = x + jnp.where(r >= k, pltpu.roll(x, k, axis=0), 0.0)
        k *= 2
    return x


def _hgrn_core_kernel(q_ref, *refs, ts):
    def tile_pair(it, st):
        st = _hgrn_core_tile(2 * it, st, q_ref, *refs, ts=ts)
        return _hgrn_core_tile(2 * it + 1, st, q_ref, *refs, ts=ts)

    lax.fori_loop(0, q_ref.shape[1] // (2 * ts), tile_pair,
                  jnp.zeros((HGRN_HEAD_DIM, HGRN_HEAD_DIM), F32))


def _hgrn_core_tile(s, st, q_ref, lf_ref, k_ref, i_ref, sg_ref, og_ref, o_ref, *, ts):
    C = HGRN_CHUNK
    r0 = pl.multiple_of(s * ts, ts)
    rows = pl.ds(r0, ts)
    b_all = _chunk_cumsum(lf_ref[0, rows, :])
    q_all = q_ref[0, rows, :].astype(F32)
    k_all = k_ref[0, rows, :].astype(F32)
    q_dec_all = (q_all * jnp.exp(b_all)).astype(BF16)
    k_inv_all = (k_all * jnp.exp(-b_all)).astype(BF16)
    iv_all = i_ref[0, rows, :]
    crow = lax.broadcasted_iota(jnp.int32, (C, C), 0)
    ccol = lax.broadcasted_iota(jnp.int32, (C, C), 1)

    n_chunks = ts // C
    sls = [slice(n * C, (n + 1) * C) for n in range(n_chunks)]
    b_lasts = [b_all[(n + 1) * C - 1:(n + 1) * C, :] for n in range(n_chunks)]
    a_list, ut_list = [], []
    for n, sl in enumerate(sls):
        a_list.append(lax.dot_general(q_dec_all[sl, :], k_inv_all[sl, :], (((1,), (1,)), ((), ())),
                                      preferred_element_type=F32))
        k_state = (k_all[sl, :] * jnp.exp(b_lasts[n] - b_all[sl, :])).astype(BF16)
        ut_list.append(lax.dot_general(iv_all[sl, :], k_state, (((0,), (0,)), ((), ())),
                                       preferred_element_type=F32))
    st_list = []
    for n in range(n_chunks):
        st_list.append(st.astype(BF16))
        st = jnp.exp(b_lasts[n]) * st + ut_list[n]
    outs = []
    for n, sl in enumerate(sls):
        a = jnp.where(ccol <= crow, a_list[n], 0.0).astype(BF16)
        o = jnp.dot(a, iv_all[sl, :], preferred_element_type=F32)
        outs.append(o + lax.dot_general(q_dec_all[sl, :], st_list[n], (((1,), (1,)), ((), ())),
                                        preferred_element_type=F32))
    o = jnp.concatenate(outs, axis=0)
    o = o * lax.rsqrt(jnp.mean(o * o, axis=-1, keepdims=True) + RMS_EPS)
    o = o * og_ref[...] * sg_ref[0, rows, :].astype(F32)
    o_ref[0, rows, :] = o.astype(BF16)
    return st


def _hgrn_core(q, lf, k, iv, sg, og, *, ts):
    B, S, D = q.shape
    H = D // HGRN_HEAD_DIM
    seq = pl.BlockSpec((1, S, HGRN_HEAD_DIM), lambda b, h: (b, 0, h))
    return pl.pallas_call(
        functools.partial(_hgrn_core_kernel, ts=ts),
        grid=(B, H),
        in_specs=[seq] * 5 + [pl.BlockSpec((1, HGRN_HEAD_DIM), lambda b, h: (0, h))],
        out_specs=seq,
        out_shape=jax.ShapeDtypeStruct((B, S, D), BF16),
        compiler_params=pltpu.CompilerParams(
            dimension_semantics=("arbitrary", "arbitrary"), vmem_limit_bytes=VMEM_LIMIT),
        name="hgrn_core",
    )(q, lf, k, iv, sg, og)


def kernel(x, att_norm_g, att_w_in, att_b_f, att_w_out, hgrn_norm_g, hgrn_w_in, hgrn_lb_logits,
           hgrn_onorm_g, hgrn_w_out, ffn_norm_g, ffn_w_up, ffn_conv_w, ffn_conv_b, ffn_w_down,
           final_norm_g):
    B, S, D = x.shape
    depth = ffn_norm_g.shape[0]
    assert D == FOX_HEADS * FOX_HEAD_DIM and FOX_HEADS * AUG_LANES == LANES
    assert S % ROW_TILE == 0 and S % ATTN_TILE == 0
    assert S % (2 * HGRN_TILE) == 0 and HGRN_TILE % HGRN_CHUNK == 0
    assert (ffn_w_up.shape[2] // 2) % FFN_CHUNK == 0

    sm = jax.nn.softmax(hgrn_lb_logits.astype(F32), axis=0)
    lower_bounds = jnp.cumsum(sm, axis=0) - sm[0:1]
    fg = final_norm_g.reshape(1, D)
    hsum = (jnp.arange(D)[:, None] // FOX_HEAD_DIM
            == jnp.arange(LANES)[None, :] // AUG_LANES).astype(BF16)

    wup, taps, wd = _ffn_weights(ffn_w_up, ffn_conv_w, ffn_conv_b, ffn_w_down)

    for layer in range(depth):
        j = layer // 2
        if layer % 2 == 0:
            order = jnp.argsort(att_b_f[j])
            cols = (order[:, None] * FOX_HEAD_DIM + jnp.arange(FOX_HEAD_DIM)[None, :]).reshape(-1)
            w_in = att_w_in[j]
            wqkv = jnp.take(w_in[:, :3 * D].reshape(D, 3, FOX_HEADS, FOX_HEAD_DIM), order,
                            axis=2).reshape(D, 3 * D).astype(BF16)
            wf = jnp.repeat(jnp.take(w_in[:, 3 * D:], order, axis=1), AUG_LANES,
                            axis=1).astype(BF16)
            bf = jnp.repeat(jnp.take(att_b_f[j], order), AUG_LANES).reshape(1, LANES)
            qkv, augq, augk, stats = _fox_in(x, att_norm_g[j].reshape(1, D), wqkv, wf, bf, hsum,
                                             ts=ATTN_TILE)
            o = _fox_attn(_attn_block_counts(stats), qkv, augq, augk, t=ATTN_TILE)
            w_out = jnp.take(att_w_out[j], cols, axis=0)
        else:
            q, lf, k, iv, sg = _hgrn_in(x, hgrn_norm_g[j].reshape(1, D), hgrn_w_in[j].astype(BF16),
                                        lower_bounds[layer].reshape(1, D), ts=HGRN_TILE)
            o = _hgrn_core(q, lf, k, iv, sg, hgrn_onorm_g[j].reshape(1, D), ts=HGRN_TILE)
            w_out = hgrn_w_out[j]
        x = _ffn(x, o, w_out.astype(BF16), ffn_norm_g[layer].reshape(1, D), wup, taps, wd, fg,
                 layer=layer, ts=ROW_TILE, final=(layer == depth - 1))
    return x
```

```python
import functools
import math

import jax
import jax.numpy as jnp
from jax import lax
from jax.experimental import pallas as pl
from jax.experimental.pallas import tpu as pltpu

F32 = jnp.float32
BF16 = jnp.bfloat16

RMS_EPS = 1e-6
FOX_HEADS = 16
FOX_HEAD_DIM = 64
HGRN_HEAD_DIM = 128
HGRN_CHUNK = 64
CONV_WIDTH = 3
LOG2E = math.log2(math.e)

LANES = 128
AUG_LANES = 8
NEG_BIG = -1e30
NORM_SLACK = 1.01
EXP2_ZERO_GAP = 152.0
VMEM_LIMIT = 56 * 1024 * 1024

ROW_TILE = 512
ATTN_TILE = 512
ATTN_STATIC_BLOCKS = 8
HGRN_TILE = 1024
FFN_CHUNK = 256
FFN_UP_LEAD = 4
FFN_STAGES = FFN_UP_LEAD + 1


def _rmsnorm(x, g):
    ms = jnp.mean(x * x, axis=-1, keepdims=True)
    return x * lax.rsqrt(ms + RMS_EPS) * g


def _split3(x):
    hi = x.astype(BF16)
    r = x - hi.astype(F32)
    mid = r.astype(BF16)
    lo = (r - mid.astype(F32)).astype(BF16)
    return hi, mid, lo


def _silu(x):
    return x * (1.0 / (1.0 + jnp.exp(-x)))


def _tile_lanes(x, n):
    return jnp.concatenate([x] * n, axis=1) if n > 1 else x


def _fox_in_kernel(x_ref, g_ref, wqkv_ref, wf_ref, bf_ref, hsum_ref, qkv_ref, augq_ref, augk_ref,
                   stats_ref, carry_ref, *, d_model, q_scale):
    ts = x_ref.shape[1]

    @pl.when(pl.program_id(1) == 0)
    def _():
        carry_ref[...] = jnp.zeros_like(carry_ref)

    h = _rmsnorm(x_ref[0], g_ref[...]).astype(BF16)

    z = jnp.dot(h, wf_ref[...], preferred_element_type=F32) + bf_ref[...]
    logf = jnp.minimum(z, 0.0) - jnp.log1p(jnp.exp(-jnp.abs(z)))
    lc = logf * LOG2E
    row = lax.broadcasted_iota(jnp.int32, (ts, ts), 0)
    col = lax.broadcasted_iota(jnp.int32, (ts, ts), 1)
    tri = jnp.where(col <= row, 1.0, 0.0).astype(BF16)
    hi, mid, lo = _split3(lc)
    cs = (jnp.dot(tri, hi, preferred_element_type=F32)
          + jnp.dot(tri, mid, preferred_element_type=F32)
          + jnp.dot(tri, lo, preferred_element_type=F32)) + carry_ref[...]
    carry_ref[...] = cs[ts - 1:ts, :]

    c_hi, c_mid, c_lo = (t.astype(F32) for t in _split3(cs))
    j = lax.broadcasted_iota(jnp.int32, (ts, LANES), 1) & (AUG_LANES - 1)
    augq = jnp.where(j == 0, c_hi, jnp.where(j == 1, c_mid, jnp.where(j == 2, c_lo,
                     jnp.where(j < 6, 1.0, 0.0))))
    augk = jnp.where(j < 3, 1.0, jnp.where(j == 3, -c_hi, jnp.where(j == 4, -c_mid,
                     jnp.where(j == 5, -c_lo, 0.0))))
    augq_ref[0] = augq.astype(BF16)
    augk_ref[0] = augk.astype(BF16)

    def project(c):
        acc = jnp.dot(h, wqkv_ref[:, c * d_model:(c + 1) * d_model], preferred_element_type=F32)
        if c == 0:
            acc = acc * q_scale
        yb = acc.astype(BF16)
        qkv_ref[0, :, c * d_model:(c + 1) * d_model] = yb
        return yb.astype(F32)

    stored = [project(0), project(1)]
    norm2 = [jnp.dot(jnp.square(y).astype(BF16), hsum_ref[...], preferred_element_type=F32)
             * NORM_SLACK for y in stored]
    diag = jnp.dot((stored[0] * stored[1]).astype(BF16), hsum_ref[...], preferred_element_type=F32)
    diag_lb = diag - jnp.sqrt(norm2[0] * norm2[1]) * (2.0 ** -8)
    project(2)

    stats_ref[0, 0] = jnp.concatenate(
        [jnp.max(norm2[0], axis=0, keepdims=True), jnp.max(norm2[1], axis=0, keepdims=True),
         jnp.max(cs - diag_lb, axis=0, keepdims=True), cs[ts - 1:ts, :],
         jnp.zeros((4, LANES), F32)], axis=0)


def _fox_in(x, g, wqkv, wf, bf, hsum, *, ts):
    B, S, D = x.shape
    q_scale = LOG2E / math.sqrt(FOX_HEAD_DIM)
    const = lambda b, s: (0, 0)
    return pl.pallas_call(
        functools.partial(_fox_in_kernel, d_model=D, q_scale=q_scale),
        grid=(B, S // ts),
        in_specs=[
            pl.BlockSpec((1, ts, D), lambda b, s: (b, s, 0)),
            pl.BlockSpec((1, D), const),
            pl.BlockSpec((D, 3 * D), const),
            pl.BlockSpec((D, LANES), const),
            pl.BlockSpec((1, LANES), const),
            pl.BlockSpec((D, LANES), const),
        ],
        out_specs=[
            pl.BlockSpec((1, ts, 3 * D), lambda b, s: (b, s, 0)),
            pl.BlockSpec((1, ts, LANES), lambda b, s: (b, s, 0)),
            pl.BlockSpec((1, ts, LANES), lambda b, s: (b, s, 0)),
            pl.BlockSpec((1, 1, 8, LANES), lambda b, s: (b, s, 0, 0)),
        ],
        out_shape=[
            jax.ShapeDtypeStruct((B, S, 3 * D), BF16),
            jax.ShapeDtypeStruct((B, S, LANES), BF16),
            jax.ShapeDtypeStruct((B, S, LANES), BF16),
            jax.ShapeDtypeStruct((B, S // ts, 8, LANES), F32),
        ],
        scratch_shapes=[pltpu.VMEM((1, LANES), F32)],
        compiler_params=pltpu.CompilerParams(
            dimension_semantics=("arbitrary", "arbitrary"), vmem_limit_bytes=VMEM_LIMIT),
        name="fox_in",
    )(x, g, wqkv, wf, bf, hsum)


def _attn_block_counts(stats):
    B, nt = stats.shape[0], stats.shape[1]
    per_head = stats[:, :, :4, ::AUG_LANES]
    qn2, kn2, c_minus_diag, c_last = (per_head[:, :, r] for r in range(4))
    kn2_max = jnp.max(kn2, axis=1, keepdims=True)
    bound = jnp.sqrt(qn2 * kn2_max) + c_minus_diag + 1.0
    gap = bound[:, :, None, :] - c_last[:, None, :, :]
    kk = lax.broadcasted_iota(jnp.int32, (nt, nt), 1)
    ii = lax.broadcasted_iota(jnp.int32, (nt, nt), 0)
    need = (gap > -EXP2_ZERO_GAP) & (kk < ii)[None, :, :, None]
    oldest = jnp.min(jnp.where(need, kk[None, :, :, None], nt), axis=2)
    count = jnp.maximum(jnp.arange(nt)[None, :, None] - oldest, 0) + 1
    pairs = jnp.max(count.reshape(B, nt, -1, 2), axis=-1)
    return jnp.transpose(pairs, (0, 2, 1)).reshape(-1).astype(jnp.int32)


def _fox_attn_kernel(nblk_ref, q_ref, *refs, t):
    j = pl.program_id(1)
    nq = q_ref.shape[1] // t

    def tile(i, carry):
        n = nblk_ref[(pl.program_id(0) * pl.num_programs(1) + j) * nq + i]
        _fox_attn_tile(i, n, j, q_ref, *refs, t=t)
        return carry

    lax.fori_loop(0, nq, tile, 0)


def _fox_attn_tile(i, n, j, q_ref, augq_ref, k_ref, v_ref, augk_ref, o_ref, lhs_ref, m_ref,
                   acc_ref, sa_ref, sb_ref, pa_ref, pb_ref, ala_ref, alb_ref, mxa_ref, mxb_ref,
                   *, t):
    lane = lax.broadcasted_iota(jnp.int32, (t, LANES), 1)
    q_rows = pl.ds(pl.multiple_of(i * t, t), t)
    q2 = q_ref[0, q_rows, :]
    aq = augq_ref[0, q_rows, :]
    zero = jnp.zeros((), BF16)
    for hh in range(2):
        lo = (2 * j + hh) * AUG_LANES
        qm = jnp.where((lane >= FOX_HEAD_DIM * hh) & (lane < FOX_HEAD_DIM * (hh + 1)), q2, zero)
        am = jnp.where((lane >= lo) & (lane < lo + AUG_LANES), aq, zero)
        lhs_ref[hh] = jnp.concatenate([qm, am], axis=1)
        m_ref[hh] = jnp.full((t, LANES), NEG_BIG, F32)
        acc_ref[hh] = jnp.zeros((t, LANES), F32)

    one = jnp.ones((), BF16)
    row = lax.broadcasted_iota(jnp.int32, (t, t), 0)
    col = lax.broadcasted_iota(jnp.int32, (t, t), 1)

    def scores(jb, buf, masked=False):
        s_ref, mx_ref, _, _ = buf
        off = pl.multiple_of((i - jb) * t, t)
        rhs = jnp.concatenate([k_ref[0, pl.ds(off, t), :], augk_ref[0, pl.ds(off, t), :]], axis=1)
        for hh in range(2):
            s = lax.dot_general(lhs_ref[hh], rhs, (((1,), (1,)), ((), ())),
                                preferred_element_type=F32)
            if masked:
                s = jnp.where(col <= row, s, NEG_BIG)
            s_ref[hh] = s
            mx = s[:, :LANES]
            for c in range(1, t // LANES):
                mx = jnp.maximum(mx, s[:, c * LANES:(c + 1) * LANES])
            mx_ref[hh] = mx

    def softmax(buf):
        s_ref, mx_ref, p_ref, al_ref = buf
        for hh in range(2):
            m_prev = m_ref[hh]
            m_new = jnp.maximum(m_prev, jnp.max(mx_ref[hh], axis=1, keepdims=True))
            al_ref[hh] = jnp.exp2(m_prev - m_new)
            p_ref[hh] = jnp.exp2(s_ref[hh] - _tile_lanes(m_new, t // LANES)).astype(BF16)
            m_ref[hh] = m_new

    def values(jb, buf):
        _, _, p_ref, al_ref = buf
        off = pl.multiple_of((i - jb) * t, t)
        v2 = v_ref[0, pl.ds(off, t), :]
        for hh in range(2):
            own = (lane >= FOX_HEAD_DIM * hh) & (lane < FOX_HEAD_DIM * (hh + 1))
            pv = jnp.dot(p_ref[hh], jnp.where(own, v2, one), preferred_element_type=F32)
            acc_ref[hh] = al_ref[hh] * acc_ref[hh] + pv

    buf_a = (sa_ref, mxa_ref, pa_ref, ala_ref)
    buf_b = (sb_ref, mxb_ref, pb_ref, alb_ref)

    def run_static(n0):
        bufs = (buf_a, buf_b)
        for tau in range(n0 + 2):
            if 0 <= tau - 2 < n0:
                values(tau - 2, bufs[tau % 2])
            if 0 <= tau - 1 < n0:
                softmax(bufs[(tau - 1) % 2])
            if tau < n0:
                scores(tau, bufs[tau % 2], masked=(tau == 0))

    for n0 in range(1, ATTN_STATIC_BLOCKS + 1):
        pl.when(n == n0)(functools.partial(run_static, n0))

    def step_pair(u, carry):
        tau = 2 + 2 * u
        values(tau - 2, buf_a)
        softmax(buf_b)
        scores(tau, buf_a)
        values(tau - 1, buf_b)
        softmax(buf_a)
        scores(tau + 1, buf_b)
        return carry

    def drain_even():
        values(n - 2, buf_a)
        softmax(buf_b)
        values(n - 1, buf_b)

    def drain_odd():
        values(n - 3, buf_a)
        softmax(buf_b)
        scores(n - 1, buf_a)
        values(n - 2, buf_b)
        softmax(buf_a)
        values(n - 1, buf_a)

    @pl.when(n > ATTN_STATIC_BLOCKS)
    def _():
        scores(0, buf_a, masked=True)
        softmax(buf_a)
        scores(1, buf_b)
        pairs = (n - 2) // 2
        lax.fori_loop(0, pairs - 1, step_pair, 0)
        for parity, drain in ((0, drain_even), (1, drain_odd)):
            @pl.when(n % 2 == parity)
            def _():
                step_pair(pairs - 1, 0)
                drain()

    out_a = acc_ref[0] / pltpu.roll(acc_ref[0], FOX_HEAD_DIM, axis=1)
    out_b = acc_ref[1] / pltpu.roll(acc_ref[1], FOX_HEAD_DIM, axis=1)
    o_ref[0, q_rows, :] = jnp.where(lane < FOX_HEAD_DIM, out_a, out_b).astype(BF16)


def _fox_attn(nblk, qkv, augq, augk, *, t):
    B, S, D3 = qkv.shape
    D = D3 // 3
    nb = D // LANES
    grid_spec = pltpu.PrefetchScalarGridSpec(
        num_scalar_prefetch=1,
        grid=(B, nb),
        in_specs=[
            pl.BlockSpec((1, S, LANES), lambda b, j, nblk: (b, 0, j)),
            pl.BlockSpec((1, S, LANES), lambda b, j, nblk: (b, 0, 0)),
            pl.BlockSpec((1, S, LANES), lambda b, j, nblk: (b, 0, nb + j)),
            pl.BlockSpec((1, S, LANES), lambda b, j, nblk: (b, 0, 2 * nb + j)),
            pl.BlockSpec((1, S, LANES), lambda b, j, nblk: (b, 0, 0)),
        ],
        out_specs=pl.BlockSpec((1, S, LANES), lambda b, j, nblk: (b, 0, j)),
        scratch_shapes=[
            pltpu.VMEM((2, t, 2 * LANES), BF16),
            pltpu.VMEM((2, t, LANES), F32),
            pltpu.VMEM((2, t, LANES), F32),
            pltpu.VMEM((2, t, t), F32),
            pltpu.VMEM((2, t, t), F32),
            pltpu.VMEM((2, t, t), BF16),
            pltpu.VMEM((2, t, t), BF16),
            pltpu.VMEM((2, t, LANES), F32),
            pltpu.VMEM((2, t, LANES), F32),
            pltpu.VMEM((2, t, LANES), F32),
            pltpu.VMEM((2, t, LANES), F32),
        ],
    )
    return pl.pallas_call(
        functools.partial(_fox_attn_kernel, t=t),
        grid_spec=grid_spec,
        out_shape=jax.ShapeDtypeStruct((B, S, D), BF16),
        compiler_params=pltpu.CompilerParams(
            dimension_semantics=("arbitrary", "arbitrary"), vmem_limit_bytes=VMEM_LIMIT),
        name="fox_attn",
    )(nblk, qkv, augq, qkv, qkv, augk)


def _ffn_kernel(x_ref, o_ref, wout_ref, g_ref, wup_ref, taps_ref, wd_ref, fg_ref,
                out_ref, tail_ref, x1_ref, h_ref, acc_ref, u_ref, z_ref, *, final):
    ts = x_ref.shape[1]
    ffn_dim = wd_ref.shape[0]
    n_chunks = ffn_dim // FFN_CHUNK

    def cols(c, stream=0):
        lo = stream * ffn_dim + c * FFN_CHUNK
        return slice(lo, lo + FFN_CHUNK)

    @pl.when(pl.program_id(1) == 0)
    def _():
        tail_ref[...] = jnp.zeros_like(tail_ref)

    x1 = x_ref[0] + jnp.dot(o_ref[0], wout_ref[...], preferred_element_type=F32)
    x1_ref[...] = x1
    h_ref[...] = _rmsnorm(x1, g_ref[...]).astype(BF16)
    acc_ref[...] = jnp.zeros_like(acc_ref)
    def up(c):
        h = h_ref[...]
        for stream in range(2):
            u = jnp.dot(h, wup_ref[:, cols(c, stream)], preferred_element_type=F32)
            u_ref[c % FFN_STAGES, stream, 0:8, :] = tail_ref[stream, c]
            u_ref[c % FFN_STAGES, stream, 8:, :] = u
            tail_ref[stream, c] = u[ts - 8:, :]

    def conv(c, stream, cw):
        u0 = u_ref[c % FFN_STAGES, stream, 8:8 + ts, :]
        u1 = u_ref[c % FFN_STAGES, stream, 7:7 + ts, :]
        u2 = u_ref[c % FFN_STAGES, stream, 6:6 + ts, :]
        return cw[3:4, :] + cw[0:1, :] * u2 + cw[1:2, :] * u1 + cw[2:3, :] * u0

    def act(c):
        gate = conv(c, 0, taps_ref[:, cols(c, 0)])
        val = conv(c, 1, taps_ref[:, cols(c, 1)])
        z_ref[c % FFN_STAGES] = (_silu(gate) * val).astype(BF16)

    def down(c):
        acc_ref[...] += jnp.dot(z_ref[c % FFN_STAGES], wd_ref[cols(c), :],
                                preferred_element_type=F32)

    for c in range(min(FFN_UP_LEAD, n_chunks)):
        up(c)
    for c in range(min(FFN_UP_LEAD - 1, n_chunks)):
        act(c)
    for c in range(n_chunks):
        if c + FFN_UP_LEAD < n_chunks:
            up(c + FFN_UP_LEAD)
        if c + FFN_UP_LEAD - 1 < n_chunks:
            act(c + FFN_UP_LEAD - 1)
        down(c)
    y = x1_ref[...] + acc_ref[...]
    if final:
        y = _rmsnorm(y, fg_ref[...])
    out_ref[0] = y


def _ffn(x, o, wout, g, wup, taps, wd, fg, *, layer, ts, final):
    B, S, D = x.shape
    n_chunks = wd.shape[1] // FFN_CHUNK
    c2 = lambda b, s: (0, 0)
    of_layer = lambda b, s: (layer, 0, 0)
    return pl.pallas_call(
        functools.partial(_ffn_kernel, final=final),
        grid=(B, S // ts),
        in_specs=[
            pl.BlockSpec((1, ts, D), lambda b, s: (b, s, 0)),
            pl.BlockSpec((1, ts, D), lambda b, s: (b, s, 0)),
            pl.BlockSpec((D, D), c2),
            pl.BlockSpec((1, D), c2),
            pl.BlockSpec((None,) + wup.shape[1:], of_layer),
            pl.BlockSpec((None,) + taps.shape[1:], of_layer),
            pl.BlockSpec((None,) + wd.shape[1:], of_layer),
            pl.BlockSpec((1, D), c2),
        ],
        out_specs=pl.BlockSpec((1, ts, D), lambda b, s: (b, s, 0)),
        out_shape=jax.ShapeDtypeStruct((B, S, D), F32),
        scratch_shapes=[
            pltpu.VMEM((2, n_chunks, 8, FFN_CHUNK), F32),
            pltpu.VMEM((ts, D), F32),
            pltpu.VMEM((ts, D), BF16),
            pltpu.VMEM((ts, D), F32),
            pltpu.VMEM((FFN_STAGES, 2, 8 + ts, FFN_CHUNK), F32),
            pltpu.VMEM((FFN_STAGES, ts, FFN_CHUNK), BF16),
        ],
        compiler_params=pltpu.CompilerParams(
            dimension_semantics=("arbitrary", "arbitrary"), vmem_limit_bytes=VMEM_LIMIT),
        name="ffn_final" if final else "ffn",
    )(x, o, wout, g, wup, taps, wd, fg)


def _ffn_weights(w_up, conv_w, conv_b, w_down):
    depth, _, f2 = conv_w.shape
    taps = jnp.concatenate(
        [conv_w, conv_b[:, None, :], jnp.zeros((depth, 8 - CONV_WIDTH - 1, f2), F32)], axis=1)
    return w_up.astype(BF16), taps, w_down.astype(BF16)


def _hgrn_in_kernel(x_ref, g_ref, w_ref, lb_ref, q_ref, lf_ref, k_ref, i_ref, sg_ref, *, d_model):
    h = _rmsnorm(x_ref[0], g_ref[...]).astype(BF16)
    D = d_model
    half = D // 2

    def proj(section, hf):
        lo = section * D + hf * half
        return jnp.dot(h, w_ref[:, lo:lo + half], preferred_element_type=F32)

    for hf in range(2):
        cols = slice(hf * half, (hf + 1) * half)
        q_ref[0, :, cols] = _silu(proj(0, hf)).astype(BF16)
    for hf in range(2):
        cols = slice(hf * half, (hf + 1) * half)
        lb = lb_ref[:, cols]
        f = lb + (1.0 - lb) * (1.0 / (1.0 + jnp.exp(-proj(1, hf))))
        lf_ref[0, :, cols] = jnp.log(f)
        k_ref[0, :, cols] = (1.0 - f).astype(BF16)
    for hf in range(2):
        cols = slice(hf * half, (hf + 1) * half)
        sg_ref[0, :, cols] = _silu(proj(3, hf)).astype(BF16)
    for hf in range(2):
        cols = slice(hf * half, (hf + 1) * half)
        i_ref[0, :, cols] = proj(2, hf).astype(BF16)


def _hgrn_in(x, g, w, lb, *, ts):
    B, S, D = x.shape
    const = lambda b, s: (0, 0)
    tile = pl.BlockSpec((1, ts, D), lambda b, s: (b, s, 0))
    return pl.pallas_call(
        functools.partial(_hgrn_in_kernel, d_model=D),
        grid=(B, S // ts),
        in_specs=[tile, pl.BlockSpec((1, D), const), pl.BlockSpec((D, 4 * D), const),
                  pl.BlockSpec((1, D), const)],
        out_specs=[tile] * 5,
        out_shape=[
            jax.ShapeDtypeStruct((B, S, D), BF16),
            jax.ShapeDtypeStruct((B, S, D), F32),
            jax.ShapeDtypeStruct((B, S, D), BF16),
            jax.ShapeDtypeStruct((B, S, D), BF16),
            jax.ShapeDtypeStruct((B, S, D), BF16),
        ],
        compiler_params=pltpu.CompilerParams(
            dimension_semantics=("arbitrary", "arbitrary"), vmem_limit_bytes=VMEM_LIMIT),
        name="hgrn_in",
    )(x, g, w, lb)


def _chunk_cumsum(x):
    r = lax.broadcasted_iota(jnp.int32, x.shape, 0) & (HGRN_CHUNK - 1)
    k = 1
    while k < HGRN_CHUNK:
        x = x + jnp.where(r >= k, pltpu.roll(x, k, axis=0), 0.0)
        k *= 2
    return x


def _hgrn_core_kernel(q_ref, *refs, ts):
    def tile_pair(it, st):
        st = _hgrn_core_tile(2 * it, st, q_ref, *refs, ts=ts)
        return _hgrn_core_tile(2 * it + 1, st, q_ref, *refs, ts=ts)

    lax.fori_loop(0, q_ref.shape[1] // (2 * ts), tile_pair,
                  jnp.zeros((HGRN_HEAD_DIM, HGRN_HEAD_DIM), F32))


def _hgrn_core_tile(s, st, q_ref, lf_ref, k_ref, i_ref, sg_ref, og_ref, o_ref, *, ts):
    C = HGRN_CHUNK
    r0 = pl.multiple_of(s * ts, ts)
    rows = pl.ds(r0, ts)
    b_all = _chunk_cumsum(lf_ref[0, rows, :])
    q_all = q_ref[0, rows, :].astype(F32)
    k_all = k_ref[0, rows, :].astype(F32)
    q_dec_all = (q_all * jnp.exp(b_all)).astype(BF16)
    k_inv_all = (k_all * jnp.exp(-b_all)).astype(BF16)
    iv_all = i_ref[0, rows, :]
    crow = lax.broadcasted_iota(jnp.int32, (C, C), 0)
    ccol = lax.broadcasted_iota(jnp.int32, (C, C), 1)

    n_chunks = ts // C
    sls = [slice(n * C, (n + 1) * C) for n in range(n_chunks)]
    b_lasts = [b_all[(n + 1) * C - 1:(n + 1) * C, :] for n in range(n_chunks)]
    a_list, ut_list = [], []
    for n, sl in enumerate(sls):
        a_list.append(lax.dot_general(q_dec_all[sl, :], k_inv_all[sl, :], (((1,), (1,)), ((), ())),
                                      preferred_element_type=F32))
        k_state = (k_all[sl, :] * jnp.exp(b_lasts[n] - b_all[sl, :])).astype(BF16)
        ut_list.append(lax.dot_general(iv_all[sl, :], k_state, (((0,), (0,)), ((), ())),
                                       preferred_element_type=F32))
    st_list = []
    for n in range(n_chunks):
        st_list.append(st.astype(BF16))
        st = jnp.exp(b_lasts[n]) * st + ut_list[n]
    outs = []
    for n, sl in enumerate(sls):
        a = jnp.where(ccol <= crow, a_list[n], 0.0).astype(BF16)
        o = jnp.dot(a, iv_all[sl, :], preferred_element_type=F32)
        outs.append(o + lax.dot_general(q_dec_all[sl, :], st_list[n], (((1,), (1,)), ((), ())),
                                        preferred_element_type=F32))
    o = jnp.concatenate(outs, axis=0)
    o = o * lax.rsqrt(jnp.mean(o * o, axis=-1, keepdims=True) + RMS_EPS)
    o = o * og_ref[...] * sg_ref[0, rows, :].astype(F32)
    o_ref[0, rows, :] = o.astype(BF16)
    return st


def _hgrn_core(q, lf, k, iv, sg, og, *, ts):
    B, S, D = q.shape
    H = D // HGRN_HEAD_DIM
    seq = pl.BlockSpec((1, S, HGRN_HEAD_DIM), lambda b, h: (b, 0, h))
    return pl.pallas_call(
        functools.partial(_hgrn_core_kernel, ts=ts),
        grid=(B, H),
        in_specs=[seq] * 5 + [pl.BlockSpec((1, HGRN_HEAD_DIM), lambda b, h: (0, h))],
        out_specs=seq,
        out_shape=jax.ShapeDtypeStruct((B, S, D), BF16),
        compiler_params=pltpu.CompilerParams(
            dimension_semantics=("arbitrary", "arbitrary"), vmem_limit_bytes=VMEM_LIMIT),
        name="hgrn_core",
    )(q, lf, k, iv, sg, og)


def kernel(x, att_norm_g, att_w_in, att_b_f, att_w_out, hgrn_norm_g, hgrn_w_in, hgrn_lb_logits,
           hgrn_onorm_g, hgrn_w_out, ffn_norm_g, ffn_w_up, ffn_conv_w, ffn_conv_b, ffn_w_down,
           final_norm_g):
    B, S, D = x.shape
    depth = ffn_norm_g.shape[0]
    assert D == FOX_HEADS * FOX_HEAD_DIM and FOX_HEADS * AUG_LANES == LANES
    assert S % ROW_TILE == 0 and S % ATTN_TILE == 0
    assert S % (2 * HGRN_TILE) == 0 and HGRN_TILE % HGRN_CHUNK == 0
    assert (ffn_w_up.shape[2] // 2) % FFN_CHUNK == 0

    sm = jax.nn.softmax(hgrn_lb_logits.astype(F32), axis=0)
    lower_bounds = jnp.cumsum(sm, axis=0) - sm[0:1]
    fg = final_norm_g.reshape(1, D)
    hsum = (jnp.arange(D)[:, None] // FOX_HEAD_DIM
            == jnp.arange(LANES)[None, :] // AUG_LANES).astype(BF16)

    wup, taps, wd = _ffn_weights(ffn_w_up, ffn_conv_w, ffn_conv_b, ffn_w_down)

    for layer in range(depth):
        j = layer // 2
        if layer % 2 == 0:
            order = jnp.argsort(att_b_f[j])
            cols = (order[:, None] * FOX_HEAD_DIM + jnp.arange(FOX_HEAD_DIM)[None, :]).reshape(-1)
            w_in = att_w_in[j]
            wqkv = jnp.take(w_in[:, :3 * D].reshape(D, 3, FOX_HEADS, FOX_HEAD_DIM), order,
                            axis=2).reshape(D, 3 * D).astype(BF16)
            wf = jnp.repeat(jnp.take(w_in[:, 3 * D:], order, axis=1), AUG_LANES,
                            axis=1).astype(BF16)
            bf = jnp.repeat(jnp.take(att_b_f[j], order), AUG_LANES).reshape(1, LANES)
            qkv, augq, augk, stats = _fox_in(x, att_norm_g[j].reshape(1, D), wqkv, wf, bf, hsum,
                                             ts=ATTN_TILE)
            o = _fox_attn(_attn_block_counts(stats), qkv, augq, augk, t=ATTN_TILE)
            w_out = jnp.take(att_w_out[j], cols, axis=0)
        else:
            q, lf, k, iv, sg = _hgrn_in(x, hgrn_norm_g[j].reshape(1, D), hgrn_w_in[j].astype(BF16),
                                        lower_bounds[layer].reshape(1, D), ts=HGRN_TILE)
            o = _hgrn_core(q, lf, k, iv, sg, hgrn_onorm_g[j].reshape(1, D), ts=HGRN_TILE)
            w_out = hgrn_w_out[j]
        x = _ffn(x, o, w_out.astype(BF16), ffn_norm_g[layer].reshape(1, D), wup, taps, wd, fg,
                 layer=layer, ts=ROW_TILE, final=(layer == depth - 1))
    return x
```
